```python
import jax, jax.numpy as jnp
from jax import lax
import numpy as np

D_MODEL = 2048
BATCH = 8
SEQ = 4096
DEPTH = 2

HEAD_DIM = 128
D_MIX = D_MODEL
N_HEADS_A = 10
N_HEADS_B = 6
N_HEADS_C = 8
N_HEADS_D = 8
IDX_HEADS = 16
IDX_DIM = 64
TOPK_MAX = 256
DILATED = ((128, 1), (512, 4), (2048, 16))
BLOCK = 128
ROPE_THETA = 10000.0
EPS = 1e-6

A_Q = N_HEADS_A * HEAD_DIM
IDX_Q = IDX_HEADS * IDX_DIM
B_W = N_HEADS_B * HEAD_DIM
C_W = N_HEADS_C * HEAD_DIM
D_W = N_HEADS_D * HEAD_DIM
SPLIT0 = (A_Q, HEAD_DIM, HEAD_DIM, IDX_Q, IDX_DIM, IDX_HEADS, B_W, B_W, B_W, D_MIX)
SPLIT1 = (C_W, C_W, C_W, D_W, D_W, D_W, N_HEADS_D, D_MIX)
IN0 = sum(SPLIT0)
IN1 = sum(SPLIT1)

kernel_name = 'hybrid_dsa_dilated_stickbreak_fox'

F32 = jnp.float32


def rmsnorm(x, g):
    xf = x.astype(F32)
    y = xf * lax.rsqrt(jnp.mean(xf * xf, axis=-1, keepdims=True) + EPS)
    return (y * g.astype(F32)).astype(x.dtype)


def rope(t, pos):
    half = t.shape[-1] // 2
    inv = ROPE_THETA ** (-jnp.arange(half, dtype=F32) / half)
    ang = pos.astype(F32)[:, None] * inv[None, :]
    cos = jnp.cos(ang)[None, :, None, :]
    sin = jnp.sin(ang)[None, :, None, :]
    tf = t.astype(F32)
    t1, t2 = tf[..., :half], tf[..., half:]
    return jnp.concatenate([t1 * cos - t2 * sin, t2 * cos + t1 * sin], axis=-1).astype(t.dtype)


def split_cols(t, sizes):
    offs, acc = [], 0
    for s in sizes[:-1]:
        acc += s
        offs.append(acc)
    return jnp.split(t, offs, axis=-1)


def blocks_to_seq(out):
    nb, bsz, q, h, d = out.shape
    return out.transpose(1, 0, 2, 3, 4).reshape(bsz, nb * q, h, d)


def dsa_attention(q, k, v, iq, ik, iw):
    S = q.shape[1]
    topk = min(TOPK_MAX, S // 4)
    key_pos = jnp.arange(S)
    ikf = ik.astype(F32)

    def block(b):
        start = b * BLOCK
        qb = lax.dynamic_slice_in_dim(q, start, BLOCK, axis=1)
        iqb = lax.dynamic_slice_in_dim(iq, start, BLOCK, axis=1).astype(F32)
        iwb = lax.dynamic_slice_in_dim(iw, start, BLOCK, axis=1).astype(F32)
        qpos = start + jnp.arange(BLOCK)
        dots = jnp.einsum('bqhd,bsd->bqhs', iqb, ikf) * (IDX_DIM ** -0.5)
        score = jnp.einsum('bqh,bqhs->bqs', iwb * (IDX_HEADS ** -0.5), jax.nn.relu(dots))
        causal = key_pos[None, :] <= qpos[:, None]
        score = jnp.where(causal[None], score, -jnp.inf)
        _, sel = lax.top_k(score, topk)
        valid = sel <= qpos[None, :, None]
        ks = jax.vmap(lambda kk, ii: kk[ii])(k, sel)
        vs = jax.vmap(lambda vv, ii: vv[ii])(v, sel)
        logits = jnp.einsum('bqhd,bqkd->bqhk', qb, ks).astype(F32) * (HEAD_DIM ** -0.5)
        logits = jnp.where(valid[:, :, None, :], logits, -jnp.inf)
        p = jax.nn.softmax(logits, axis=-1)
        return jnp.einsum('bqhk,bqkd->bqhd', p.astype(v.dtype), vs)

    return blocks_to_seq(lax.map(block, jnp.arange(S // BLOCK)))


def strided_band(q, k, v, dil, steps):
    bsz, S, H, hd = q.shape
    L = S // dil
    nb = -(-L // BLOCK)
    Lp = nb * BLOCK

    def to_blocks(t):
        t = t.reshape(bsz, L, dil, H, hd).transpose(0, 2, 1, 3, 4)
        t = jnp.pad(t, ((0, 0), (0, 0), (0, Lp - L), (0, 0), (0, 0)))
        return t.reshape(bsz, dil, nb, BLOCK, H, hd)

    def with_prev(t):
        prev = jnp.pad(t, ((0, 0), (0, 0), (1, 0), (0, 0), (0, 0), (0, 0)))[:, :, :-1]
        return jnp.concatenate([prev, t], axis=3)

    qb = to_blocks(q)
    kk = with_prev(to_blocks(k))
    vv = with_prev(to_blocks(v))
    logits = jnp.einsum('brnqhd,brnkhd->brnhqk', qb, kk).astype(F32) * (HEAD_DIM ** -0.5)
    qi = jnp.arange(BLOCK)[:, None] + BLOCK
    ki = jnp.arange(2 * BLOCK)[None, :]
    dist = qi - ki
    band = (dist >= 0) & (dist <= steps)
    blk = jnp.arange(nb)[:, None, None]
    valid = band[None] & ((ki[None] >= BLOCK) | (blk > 0))
    logits = jnp.where(valid[None, None, :, None], logits, -jnp.inf)
    m = jnp.max(logits, axis=-1, keepdims=True)
    p = jnp.exp(logits - m)
    l = jnp.sum(p, axis=-1, keepdims=True)
    o = jnp.einsum('brnhqk,brnkhd->brnhqd', p, vv.astype(F32)) / l
    lse = (m + jnp.log(l))[..., 0]
    o = o.transpose(0, 1, 2, 4, 3, 5).reshape(bsz, dil, Lp, H, hd)[:, :, :L]
    o = o.transpose(0, 2, 1, 3, 4).reshape(bsz, S, H, hd)
    lse = lse.transpose(0, 1, 2, 4, 3).reshape(bsz, dil, Lp, H)[:, :, :L]
    lse = lse.transpose(0, 2, 1, 3).reshape(bsz, S, H)
    return o, lse


def dilated_attention(q, k, v):
    outs, lses = [], []
    for window, dil in DILATED:
        o, lse = strided_band(q, k, v, dil, window // dil)
        outs.append(o)
        lses.append(lse)
    alpha = jax.nn.softmax(jnp.stack(lses, axis=0), axis=0)
    o = jnp.einsum('pbsh,pbshd->bshd', alpha, jnp.stack(outs, axis=0))
    return o.astype(q.dtype)


def stick_breaking_attention(q, k, v):
    S = q.shape[1]
    key_pos = jnp.arange(S)

    def block(b):
        start = b * BLOCK
        qb = lax.dynamic_slice_in_dim(q, start, BLOCK, axis=1)
        qpos = start + jnp.arange(BLOCK)
        z = jnp.einsum('bqhd,bshd->bhqs', qb, k).astype(F32) * (HEAD_DIM ** -0.5)
        before = (key_pos[None, :] < qpos[:, None])[None, None]
        log_1m = jnp.where(before, jax.nn.log_sigmoid(-z), 0.0)
        tail = lax.cumsum(log_1m, axis=3, reverse=True) - log_1m
        w = jnp.where(before, jnp.exp(jax.nn.log_sigmoid(z) + tail), 0.0)
        return jnp.einsum('bhqs,bshd->bqhd', w.astype(v.dtype), v)

    return blocks_to_seq(lax.map(block, jnp.arange(S // BLOCK)))


def forgetting_attention(q, k, v, log_f):
    S = q.shape[1]
    key_pos = jnp.arange(S)
    c = jnp.cumsum(log_f.astype(F32), axis=1).transpose(0, 2, 1)

    def block(b):
        start = b * BLOCK
        qb = lax.dynamic_slice_in_dim(q, start, BLOCK, axis=1)
        cb = lax.dynamic_slice_in_dim(c, start, BLOCK, axis=2)
        qpos = start + jnp.arange(BLOCK)
        logits = jnp.einsum('bqhd,bshd->bhqs', qb, k).astype(F32) * (HEAD_DIM ** -0.5)
        logits = logits + (cb[..., :, None] - c[..., None, :])
        causal = (key_pos[None, :] <= qpos[:, None])[None, None]
        p = jax.nn.softmax(jnp.where(causal, logits, -jnp.inf), axis=-1)
        return jnp.einsum('bhqs,bshd->bqhd', p.astype(v.dtype), v)

    return blocks_to_seq(lax.map(block, jnp.arange(S // BLOCK)))


def layer_even(x, norm_g, w_in, w_out):
    bsz, S, _ = x.shape
    pos = jnp.arange(S)
    h = rmsnorm(x, norm_g)
    qa, ka, va, iq, ik, iw, qb, kb, vb, gate = split_cols(h @ w_in, SPLIT0)
    qa = rope(qa.reshape(bsz, S, N_HEADS_A, HEAD_DIM), pos)
    ka = rope(ka.reshape(bsz, S, 1, HEAD_DIM), pos)[:, :, 0]
    iq = rope(iq.reshape(bsz, S, IDX_HEADS, IDX_DIM), pos)
    ik = rope(ik.reshape(bsz, S, 1, IDX_DIM), pos)[:, :, 0]
    o_a = dsa_attention(qa, ka, va, iq, ik, iw)
    qb = rope(qb.reshape(bsz, S, N_HEADS_B, HEAD_DIM), pos)
    kb = rope(kb.reshape(bsz, S, N_HEADS_B, HEAD_DIM), pos)
    vb = vb.reshape(bsz, S, N_HEADS_B, HEAD_DIM)
    o_b = dilated_attention(qb, kb, vb)
    y = jnp.concatenate([o_a.reshape(bsz, S, A_Q), o_b.reshape(bsz, S, B_W)], axis=-1)
    return x + (y * jax.nn.silu(gate)) @ w_out


def layer_odd(x, norm_g, w_in, b_f, w_out):
    bsz, S, _ = x.shape
    h = rmsnorm(x, norm_g)
    qc, kc, vc, qd, kd, vd, fl, gate = split_cols(h @ w_in, SPLIT1)
    shp_c = (bsz, S, N_HEADS_C, HEAD_DIM)
    shp_d = (bsz, S, N_HEADS_D, HEAD_DIM)
    o_c = stick_breaking_attention(qc.reshape(shp_c), kc.reshape(shp_c), vc.reshape(shp_c))
    log_f = jax.nn.log_sigmoid(fl.astype(F32) + b_f.astype(F32))
    o_d = forgetting_attention(qd.reshape(shp_d), kd.reshape(shp_d), vd.reshape(shp_d), log_f)
    y = jnp.concatenate([o_c.reshape(bsz, S, C_W), o_d.reshape(bsz, S, D_W)], axis=-1)
    return x + (y * jax.nn.silu(gate)) @ w_out


def setup_inputs(seed: int = 0) -> dict:
    key = jax.random.key(seed)
    ks = jax.random.split(key, 10)
    nrm = jax.random.normal
    return {
        'x': nrm(ks[0], (BATCH, SEQ, D_MODEL), F32),
        'norm0': 1.0 + 0.02 * nrm(ks[1], (D_MODEL,), F32),
        'w_in0': nrm(ks[2], (D_MODEL, IN0), F32) * D_MODEL ** -0.5,
        'w_out0': nrm(ks[3], (D_MIX, D_MODEL), F32) * D_MIX ** -0.5,
        'norm1': 1.0 + 0.02 * nrm(ks[4], (D_MODEL,), F32),
        'w_in1': nrm(ks[5], (D_MODEL, IN1), F32) * D_MODEL ** -0.5,
        'b_f1': 2.0 + 0.5 * nrm(ks[6], (N_HEADS_D,), F32),
        'w_out1': nrm(ks[7], (D_MIX, D_MODEL), F32) * D_MIX ** -0.5,
        'norm_f': 1.0 + 0.02 * nrm(ks[8], (D_MODEL,), F32),
    }


def reference(x, norm0, w_in0, w_out0, norm1, w_in1, b_f1, w_out1, norm_f):
    for i in range(DEPTH):
        if i % 2 == 0:
            x = layer_even(x, norm0, w_in0, w_out0)
        else:
            x = layer_odd(x, norm1, w_in1, b_f1, w_out1)
    return rmsnorm(x, norm_f)
```

```python
import functools
import math

import numpy as np
import jax
import jax.numpy as jnp
from jax import lax
from jax.experimental import pallas as pl
from jax.experimental.pallas import tpu as pltpu

F32 = jnp.float32
BF16 = jnp.bfloat16
I32 = jnp.int32

HEAD_DIM = 128
N_HEADS_A = 10
N_HEADS_B = 6
N_HEADS_C = 8
N_HEADS_D = 8
IDX_HEADS = 16
IDX_DIM = 64
TOPK_MAX = 256
DILATED = ((128, 1), (512, 4), (2048, 16))
ROPE_THETA = 10000.0
EPS = 1e-6

LANES = 128
MXU_COLS = 256
VMEM_LIMIT = 56 * 1024 * 1024

NEG_MASKED = -1.0e30
NEG_INIT = -0.5e30
INT_MIN = np.int32(-2 ** 31)

L0_GATE, L0_IQ, L0_QB, L0_QA = 0, 2048, 3072, 3840
L0_KA, L0_VA, L0_IKLO, L0_IKHI, L0_KB, L0_VB = 5120, 5248, 5376, 5504, 5632, 6400
L0_WIDTH = 7168
L1_GATE, L1_QC, L1_KC, L1_VC, L1_QD, L1_KD, L1_VD, L1_FL = 0, 2048, 3072, 4096, 5120, 6144, 7168, 8192
L1_WIDTH = 8448

E_NONE, E_Q128, E_K128, E_QSCALE, E_IDX64, E_IKLO = range(6)


def _dot_nt(a, b):
    return lax.dot_general(a, b, (((1,), (1,)), ((), ())), preferred_element_type=F32)


def _dot(a, b):
    return jnp.dot(a, b, preferred_element_type=F32)


def _log_sigmoid(z):
    return jnp.minimum(z, 0.0) - jnp.log(1.0 + jnp.exp(-jnp.abs(z)))


def _any_of(j, values):
    cond = j == values[0]
    for v in values[1:]:
        cond = jnp.logical_or(cond, j == v)
    return cond


def _rope128(t, cos, sin):
    return t * cos + pltpu.roll(t, 64, 1) * sin


def _rope64(t, cos, sin_up, sin_dn):
    return t * cos + pltpu.roll(t, 96, 1) * sin_up + pltpu.roll(t, 32, 1) * sin_dn


def _inproj_kernel(*refs, blocks, small_block, use_rope):
    if use_rope:
        x_ref, g_ref, w_ref, c128, s128, c64, su64, sd64, o_ref, small_ref, h_scr = refs
    else:
        x_ref, g_ref, w_ref, o_ref, small_ref, h_scr = refs
    j = pl.program_id(1)

    @pl.when(j == 0)
    def _():
        x = x_ref[...]
        ms = jnp.mean(x * x, axis=-1, keepdims=True)
        h_scr[...] = ((x * lax.rsqrt(ms + EPS)) * g_ref[...]).astype(BF16)

    acc = _dot(h_scr[...], w_ref[...])
    scale = HEAD_DIM ** -0.5

    def half(a, kind):
        if kind == E_NONE:
            f = a
        elif kind == E_Q128:
            f = _rope128(a * scale, c128[...], s128[...])
        elif kind == E_K128:
            f = _rope128(a, c128[...], s128[...])
        elif kind == E_QSCALE:
            f = a * scale
        elif kind == E_IDX64:
            f = _rope64(a, c64[...], su64[...], sd64[...])
        elif kind == E_IKLO:
            lane = lax.broadcasted_iota(I32, a.shape, 1)
            r = _rope64(a, c64[...], su64[...], sd64[...])
            f = jnp.where(lane < IDX_DIM, r, a)
            return f, jnp.where(lane < IDX_DIM, r, 0.0).astype(BF16)
        else:
            raise ValueError(kind)
        return f, f.astype(BF16)

    groups = {}
    for jj, kinds in enumerate(blocks):
        groups.setdefault((kinds, jj == small_block), []).append(jj)
    for (kinds, is_small), js in groups.items():
        @pl.when(_any_of(j, js))
        def _(kinds=kinds, is_small=is_small):
            f0, b0 = half(acc[:, :LANES], kinds[0])
            f1, b1 = half(acc[:, LANES:], kinds[1])
            o_ref[:, :LANES] = b0
            o_ref[:, LANES:] = b1
            if is_small:
                small_ref[:, :LANES] = f0
                small_ref[:, LANES:] = f1


def _inproj(x2, g, w, tables, blocks, small_block, seq, tm):
    m, d = x2.shape
    n = w.shape[1]
    nblk = n // MXU_COLS
    assert nblk == len(blocks) and m % tm == 0 and seq % tm == 0
    use_rope = tables is not None
    per_seq = seq // tm
    in_specs = [
        pl.BlockSpec((tm, d), lambda i, j: (i, 0)),
        pl.BlockSpec((1, d), lambda i, j: (0, 0)),
        pl.BlockSpec((d, MXU_COLS), lambda i, j: (0, j)),
    ]
    args = [x2, g.reshape(1, d), w]
    if use_rope:
        for t in tables:
            in_specs.append(pl.BlockSpec((tm, LANES), lambda i, j: (i % per_seq, 0)))
            args.append(t)
    return pl.pallas_call(
        functools.partial(_inproj_kernel, blocks=tuple(blocks), small_block=small_block, use_rope=use_rope),
        grid=(m // tm, nblk),
        in_specs=in_specs,
        out_specs=[
            pl.BlockSpec((tm, MXU_COLS), lambda i, j: (i, j)),
            pl.BlockSpec((tm, MXU_COLS), lambda i, j: (i, 0)),
        ],
        out_shape=[
            jax.ShapeDtypeStruct((m, n), BF16),
            jax.ShapeDtypeStruct((m, MXU_COLS), F32),
        ],
        scratch_shapes=[pltpu.VMEM((tm, d), BF16)],
        compiler_params=pltpu.CompilerParams(
            dimension_semantics=("parallel", "arbitrary"), vmem_limit_bytes=VMEM_LIMIT),
        name="inproj",
    )(*args)


def _rope_tables(seq):
    pos = jnp.arange(seq, dtype=F32)[:, None]
    half = HEAD_DIM // 2
    inv = ROPE_THETA ** (-jnp.arange(half, dtype=F32) / half)
    ang = pos * inv[None, :]
    c128 = jnp.concatenate([jnp.cos(ang), jnp.cos(ang)], axis=1)
    s128 = jnp.concatenate([-jnp.sin(ang), jnp.sin(ang)], axis=1)
    half = IDX_DIM // 2
    inv = ROPE_THETA ** (-jnp.arange(half, dtype=F32) / half)
    ang = pos * inv[None, :]
    zero = jnp.zeros_like(ang)
    c64 = jnp.concatenate([jnp.cos(ang)] * 4, axis=1)
    su64 = jnp.concatenate([-jnp.sin(ang), zero] * 2, axis=1)
    sd64 = jnp.concatenate([zero, jnp.sin(ang)] * 2, axis=1)
    return c128, s128, c64, su64, sd64


def _dsa_kernel(iq_ref, sm_ref, iklo_ref, ikhi_ref, qa_ref, ka_ref, va_ref, o_ref,
                key_scr, w_scr, m_scr, l_scr, acc_scr, *, tq, kc, topk):
    qi = pl.program_id(1)
    nch = ((qi + 1) * tq + kc - 1) // kc
    nt = kc // LANES
    row = lax.broadcasted_iota(I32, (tq, LANES), 0) + qi * tq
    lane = lax.broadcasted_iota(I32, (tq, LANES), 1)

    wscale = (IDX_DIM ** -0.5) * (IDX_HEADS ** -0.5)
    for h in range(IDX_HEADS):
        col = sm_ref[:, IDX_DIM + h:IDX_DIM + h + 1] * wscale
        w_scr[h] = jnp.broadcast_to(col, (tq, LANES))

    def score_chunk(c, carry):
        start = pl.multiple_of(c * kc, kc)
        klo = iklo_ref[pl.ds(start, kc), :]
        khi = ikhi_ref[pl.ds(start, kc), :]
        tiles = [jnp.zeros((tq, LANES), F32) for _ in range(nt)]
        for p in range(IDX_HEADS // 2):
            iq2 = iq_ref[:, p * LANES:(p + 1) * LANES]
            d0 = _dot_nt(iq2, klo)
            d1 = _dot_nt(iq2, khi)
            w0 = w_scr[2 * p]
            w1 = w_scr[2 * p + 1]
            for t in range(nt):
                sl = slice(t * LANES, (t + 1) * LANES)
                tiles[t] = tiles[t] + w0 * jnp.maximum(d0[:, sl], 0.0) + w1 * jnp.maximum(d1[:, sl], 0.0)
        for t in range(nt):
            bits = pltpu.bitcast(tiles[t], I32)
            okey = jnp.where(bits < 0, bits ^ np.int32(0x7FFFFFFF), bits)
            s_idx = lane + (start + t * LANES)
            okey = jnp.where(s_idx <= row, okey, INT_MIN)
            key_scr[:, pl.ds(pl.multiple_of(start + t * LANES, LANES), LANES)] = okey
        return carry

    lax.fori_loop(0, nch, score_chunk, 0)

    def bit_step(it, t_u):
        bit = jnp.left_shift(np.int32(1), 31 - it)
        cand_u = t_u | bit
        cand_s = cand_u ^ INT_MIN

        def count_chunk(c, cnt):
            start = pl.multiple_of(c * kc, kc)
            for t in range(nt):
                k = key_scr[:, pl.ds(pl.multiple_of(start + t * LANES, LANES), LANES)]
                cnt = cnt + jnp.where(k >= cand_s, 1.0, 0.0)
            return cnt

        cnt = lax.fori_loop(0, nch, count_chunk, jnp.zeros((tq, LANES), F32))
        total = jnp.sum(cnt, axis=1, keepdims=True)
        return jnp.where(jnp.broadcast_to(total, (tq, LANES)) >= float(topk), cand_u, t_u)

    t_u = lax.fori_loop(0, 32, bit_step, jnp.zeros((tq, LANES), I32))
    thr = jnp.maximum(t_u ^ INT_MIN, INT_MIN + np.int32(1))

    m_scr[...] = jnp.full(m_scr.shape, NEG_INIT, F32)
    l_scr[...] = jnp.zeros(l_scr.shape, F32)
    acc_scr[...] = jnp.zeros(acc_scr.shape, F32)

    def attn_chunk(c, carry):
        start = pl.multiple_of(c * kc, kc)
        kk = ka_ref[pl.ds(start, kc), :]
        vv = va_ref[pl.ds(start, kc), :]
        sel = [key_scr[:, pl.ds(pl.multiple_of(start + t * LANES, LANES), LANES)] >= thr for t in range(nt)]
        for h in range(N_HEADS_A):
            s = _dot_nt(qa_ref[:, h * LANES:(h + 1) * LANES], kk)
            sm = [jnp.where(sel[t], s[:, t * LANES:(t + 1) * LANES], NEG_MASKED) for t in range(nt)]
            mx = sm[0]
            for t in range(1, nt):
                mx = jnp.maximum(mx, sm[t])
            m_old = m_scr[h]
            m_new = jnp.maximum(m_old, jnp.max(mx, axis=1, keepdims=True))
            alpha = jnp.exp(m_old - m_new)
            p = [jnp.exp(sm[t] - m_new) for t in range(nt)]
            ps = p[0]
            for t in range(1, nt):
                ps = ps + p[t]
            l_scr[h] = alpha * l_scr[h] + jnp.sum(ps, axis=1, keepdims=True)
            pb = jnp.concatenate(p, axis=1).astype(BF16)
            acc_scr[h] = alpha * acc_scr[h] + _dot(pb, vv)
            m_scr[h] = m_new
        return carry

    lax.fori_loop(0, nch, attn_chunk, 0)
    for h in range(N_HEADS_A):
        o_ref[:, h * LANES:(h + 1) * LANES] = (acc_scr[h] / l_scr[h]).astype(BF16)


def _dsa(main3, small3, tq, kc):
    b, s, _ = main3.shape
    topk = min(TOPK_MAX, s // 4)
    a_q = N_HEADS_A * HEAD_DIM
    iq_w = IDX_HEADS * IDX_DIM
    assert s % kc == 0 and kc % tq == 0 and s % tq == 0

    def slot(off):
        return lambda bi, qi: (bi, 0, off // LANES)

    return pl.pallas_call(
        functools.partial(_dsa_kernel, tq=tq, kc=kc, topk=topk),
        grid=(b, s // tq),
        in_specs=[
            pl.BlockSpec((None, tq, iq_w), lambda bi, qi: (bi, qi, L0_IQ // iq_w)),
            pl.BlockSpec((None, tq, MXU_COLS), lambda bi, qi: (bi, qi, 0)),
            pl.BlockSpec((None, s, LANES), slot(L0_IKLO)),
            pl.BlockSpec((None, s, LANES), slot(L0_IKHI)),
            pl.BlockSpec((None, tq, a_q), lambda bi, qi: (bi, qi, L0_QA // a_q)),
            pl.BlockSpec((None, s, LANES), slot(L0_KA)),
            pl.BlockSpec((None, s, LANES), slot(L0_VA)),
        ],
        out_specs=pl.BlockSpec((None, tq, a_q), lambda bi, qi: (bi, qi, 0)),
        out_shape=jax.ShapeDtypeStruct((b, s, a_q), BF16),
        scratch_shapes=[
            pltpu.VMEM((tq, s), I32),
            pltpu.VMEM((IDX_HEADS, tq, LANES), F32),
            pltpu.VMEM((N_HEADS_A, tq, LANES), F32),
            pltpu.VMEM((N_HEADS_A, tq, LANES), F32),
            pltpu.VMEM((N_HEADS_A, tq, LANES), F32),
        ],
        compiler_params=pltpu.CompilerParams(
            dimension_semantics=("parallel", "arbitrary"), vmem_limit_bytes=VMEM_LIMIT),
        name="dsa",
    )(main3, small3, main3, main3, main3, main3, main3)


def _dilated_kernel(q_ref, k_ref, v_ref, o_ref, qf, kf, vf, o0, o1, o2, e0, e1, e2, *, seq, blk):
    qf[...] = q_ref[...].astype(F32)
    kf[...] = k_ref[...].astype(F32)
    vf[...] = v_ref[...].astype(F32)
    o_scr = (o0, o1, o2)
    e_scr = (e0, e1, e2)
    row = lax.broadcasted_iota(I32, (blk, blk), 0)
    col = lax.broadcasted_iota(I32, (blk, blk), 1)

    for pat, (window, dil) in enumerate(DILATED):
        steps = window // dil
        length = seq // dil
        nb = length // blk
        cur_ok = jnp.logical_and(row - col >= 0, row - col <= steps)
        prev_band = jnp.logical_and(row + blk - col >= 0, row + blk - col <= steps)

        def rows(ref, n, r, dil=dil):
            start = n * (blk * dil) + r
            if dil == 1:
                return ref[pl.ds(start, blk), :]
            return ref[pl.ds(start, blk, stride=dil), :]

        def step(idx, carry, pat=pat, dil=dil, nb=nb, cur_ok=cur_ok, prev_band=prev_band, rows=rows):
            r = idx // nb
            n = idx % nb
            n_prev = jnp.maximum(n - 1, 0)
            q = rows(qf, n, r).astype(BF16)
            k_cur = rows(kf, n, r).astype(BF16)
            k_prev = rows(kf, n_prev, r).astype(BF16)
            v_cur = rows(vf, n, r).astype(BF16)
            v_prev = rows(vf, n_prev, r).astype(BF16)
            s_cur = jnp.where(cur_ok, _dot_nt(q, k_cur), NEG_MASKED)
            prev_ok = jnp.logical_and(prev_band, n > 0)
            s_prev = jnp.where(prev_ok, _dot_nt(q, k_prev), NEG_MASKED)
            m = jnp.max(jnp.maximum(s_cur, s_prev), axis=1, keepdims=True)
            p_cur = jnp.exp(s_cur - m)
            p_prev = jnp.exp(s_prev - m)
            l = jnp.sum(p_cur + p_prev, axis=1, keepdims=True)
            o = (_dot(p_cur.astype(BF16), v_cur) + _dot(p_prev.astype(BF16), v_prev)) / l
            lse = jnp.broadcast_to(m + jnp.log(l), (blk, LANES))
            start = n * (blk * dil) + r
            if dil == 1:
                o_scr[pat][pl.ds(start, blk), :] = o
                e_scr[pat][pl.ds(start, blk), :] = lse
            else:
                o_scr[pat][pl.ds(start, blk, stride=dil), :] = o
                e_scr[pat][pl.ds(start, blk, stride=dil), :] = lse
            return carry

        lax.fori_loop(0, dil * nb, step, 0)

    rows_per = 512
    def merge(i, carry):
        sl = pl.ds(pl.multiple_of(i * rows_per, rows_per), rows_per)
        es = [e[sl, :] for e in e_scr]
        mx = jnp.maximum(jnp.maximum(es[0], es[1]), es[2])
        ws = [jnp.exp(e - mx) for e in es]
        num = ws[0] * o_scr[0][sl, :] + ws[1] * o_scr[1][sl, :] + ws[2] * o_scr[2][sl, :]
        o_ref[sl, :] = (num / (ws[0] + ws[1] + ws[2])).astype(BF16)
        return carry

    lax.fori_loop(0, seq // rows_per, merge, 0)


def _dilated(main3):
    b, s, _ = main3.shape
    blk = 128
    max_dil = max(d for _, d in DILATED)
    assert s % (blk * max_dil) == 0 and s % 512 == 0

    def slot(off):
        return lambda bi, h: (bi, 0, off // LANES + h)

    return pl.pallas_call(
        functools.partial(_dilated_kernel, seq=s, blk=blk),
        grid=(b, N_HEADS_B),
        in_specs=[
            pl.BlockSpec((None, s, LANES), slot(L0_QB)),
            pl.BlockSpec((None, s, LANES), slot(L0_KB)),
            pl.BlockSpec((None, s, LANES), slot(L0_VB)),
        ],
        out_specs=pl.BlockSpec((None, s, LANES), lambda bi, h: (bi, 0, h)),
        out_shape=jax.ShapeDtypeStruct((b, s, N_HEADS_B * HEAD_DIM), BF16),
        scratch_shapes=[pltpu.VMEM((s, LANES), F32) for _ in range(9)],
        compiler_params=pltpu.CompilerParams(
            dimension_semantics=("parallel", "parallel"), vmem_limit_bytes=VMEM_LIMIT),
        name="dilated",
    )(main3, main3, main3)


def _stick_kernel(q_ref, k_ref, v_ref, o_ref, *, tq, ck, hg):
    qi = pl.program_id(2)
    nch = ((qi + 1) * tq) // ck
    jj = lax.broadcasted_iota(I32, (ck, ck + LANES), 0)
    ss = lax.broadcasted_iota(I32, (ck, ck + LANES), 1)
    tri = jnp.where(jnp.logical_or(jj > ss, ss >= ck), 1.0, 0.0).astype(BF16)
    drc = (lax.broadcasted_iota(I32, (tq, ck), 1) - lax.broadcasted_iota(I32, (tq, ck), 0))

    def chunk(it, carry):
        c = nch - 1 - it
        start = pl.multiple_of(c * ck, ck)
        before = drc < (qi * tq - c * ck)
        new = []
        for h in range(hg):
            run, acc = carry[h]
            hs = slice(h * LANES, (h + 1) * LANES)
            kk = k_ref[pl.ds(start, ck), hs]
            vv = v_ref[pl.ds(start, ck), hs]
            z = _dot_nt(q_ref[:, hs], kk)
            ls = _log_sigmoid(z)
            lm = jnp.where(before, ls - z, 0.0)
            hi = lm.astype(BF16)
            lo = (lm - hi.astype(F32)).astype(BF16)
            t2 = _dot(jnp.concatenate([hi, lo], axis=0), tri)
            t2 = t2[:tq] + t2[tq:]
            tail = t2[:, :ck]
            rowsum = t2[:, ck:]
            if ck != LANES:
                run_t = jnp.concatenate([run] * (ck // LANES), axis=1)
            else:
                run_t = run
            w = jnp.where(before, jnp.exp(ls + tail + run_t), 0.0)
            acc = acc + _dot(w.astype(BF16), vv)
            new.append((run + rowsum, acc))
        return tuple(new)

    init = tuple((jnp.zeros((tq, LANES), F32), jnp.zeros((tq, LANES), F32)) for _ in range(hg))
    out = lax.fori_loop(0, nch, chunk, init)
    for h in range(hg):
        o_ref[:, h * LANES:(h + 1) * LANES] = out[h][1].astype(BF16)


def _stick(main3, tq, ck, hg):
    b, s, _ = main3.shape
    gw = hg * HEAD_DIM
    assert N_HEADS_C % hg == 0 and tq % ck == 0 and s % tq == 0
    return pl.pallas_call(
        functools.partial(_stick_kernel, tq=tq, ck=ck, hg=hg),
        grid=(b, N_HEADS_C // hg, s // tq),
        in_specs=[
            pl.BlockSpec((None, tq, gw), lambda bi, g, qi: (bi, qi, L1_QC // gw + g)),
            pl.BlockSpec((None, s, gw), lambda bi, g, qi: (bi, 0, L1_KC // gw + g)),
            pl.BlockSpec((None, s, gw), lambda bi, g, qi: (bi, 0, L1_VC // gw + g)),
        ],
        out_specs=pl.BlockSpec((None, tq, gw), lambda bi, g, qi: (bi, qi, g)),
        out_shape=jax.ShapeDtypeStruct((b, s, N_HEADS_C * HEAD_DIM), BF16),
        compiler_params=pltpu.CompilerParams(
            dimension_semantics=("parallel", "parallel", "arbitrary"), vmem_limit_bytes=VMEM_LIMIT),
        name="stick",
    )(main3, main3, main3)


def _split3(x):
    x1 = x.astype(BF16)
    r1 = x - x1.astype(F32)
    x2 = r1.astype(BF16)
    x3 = (r1 - x2.astype(F32)).astype(BF16)
    return x1, x2, x3


def _forget_cumsum_kernel(fl_ref, b_ref, c_ref, *, heads, rows):
    x = fl_ref[...] + b_ref[...]
    lf = _log_sigmoid(x).reshape(heads * rows, LANES)
    jj = lax.broadcasted_iota(I32, (LANES, 2 * LANES), 0)
    ss = lax.broadcasted_iota(I32, (LANES, 2 * LANES), 1)
    upper = jnp.where(jnp.logical_or(jj <= ss, ss >= LANES), 1.0, 0.0).astype(BF16)
    both = sum(_dot(part, upper) for part in _split3(lf))
    within = both[:, :LANES]
    total = both[:, LANES:]
    n = heads * rows
    aa = lax.broadcasted_iota(I32, (n, n), 0)
    bb = lax.broadcasted_iota(I32, (n, n), 1)
    same_head = (aa // rows) == (bb // rows)
    lower = jnp.where(jnp.logical_and(same_head, bb < aa), 1.0, 0.0).astype(BF16)
    offs = sum(_dot(lower, part) for part in _split3(total))
    c_ref[...] = (within + offs).reshape(heads, rows, LANES)


def _forget_cumsum(fl4, bf3):
    b, heads, rows, _ = fl4.shape
    return pl.pallas_call(
        functools.partial(_forget_cumsum_kernel, heads=heads, rows=rows),
        grid=(b,),
        in_specs=[
            pl.BlockSpec((None, heads, rows, LANES), lambda bi: (bi, 0, 0, 0)),
            pl.BlockSpec((heads, 1, LANES), lambda bi: (0, 0, 0)),
        ],
        out_specs=pl.BlockSpec((None, heads, rows, LANES), lambda bi: (bi, 0, 0, 0)),
        out_shape=jax.ShapeDtypeStruct(fl4.shape, F32),
        compiler_params=pltpu.CompilerParams(dimension_semantics=("parallel",)),
        name="forget_cumsum",
    )(fl4, bf3)


def _fox_kernel(q_ref, k_ref, v_ref, crow_ref, ccol_ref, o_ref, m_scr, l_scr, acc_scr, *, tq, hg):
    qi = pl.program_id(2)
    drc = (lax.broadcasted_iota(I32, (tq, tq), 1) - lax.broadcasted_iota(I32, (tq, tq), 0))
    m_scr[...] = jnp.full(m_scr.shape, NEG_INIT, F32)
    l_scr[...] = jnp.zeros(l_scr.shape, F32)
    acc_scr[...] = jnp.zeros(acc_scr.shape, F32)

    def chunk(c, carry):
        start = pl.multiple_of(c * tq, tq)
        causal = drc <= (qi - c) * tq
        for h in range(hg):
            hs = slice(h * LANES, (h + 1) * LANES)
            s = _dot_nt(q_ref[:, hs], k_ref[pl.ds(start, tq), hs])
            s = s + (ccol_ref[h] - crow_ref[h, :, pl.ds(start, tq)])
            s = jnp.where(causal, s, NEG_MASKED)
            m_old = m_scr[h]
            m_new = jnp.maximum(m_old, jnp.max(s, axis=1, keepdims=True))
            alpha = jnp.exp(m_old - m_new)
            if tq != LANES:
                m_t = jnp.concatenate([m_new] * (tq // LANES), axis=1)
            else:
                m_t = m_new
            p = jnp.exp(s - m_t)
            l_scr[h] = alpha * l_scr[h] + jnp.sum(p, axis=1, keepdims=True)
            acc_scr[h] = alpha * acc_scr[h] + _dot(p.astype(BF16), v_ref[pl.ds(start, tq), hs])
            m_scr[h] = m_new
        return carry

    lax.fori_loop(0, qi + 1, chunk, 0)
    for h in range(hg):
        o_ref[:, h * LANES:(h + 1) * LANES] = (acc_scr[h] / l_scr[h]).astype(BF16)


def _fox(main3, crow, ccol, tq, hg):
    b, s, _ = main3.shape
    gw = hg * HEAD_DIM
    assert N_HEADS_D % hg == 0 and s % tq == 0
    return pl.pallas_call(
        functools.partial(_fox_kernel, tq=tq, hg=hg),
        grid=(b, N_HEADS_D // hg, s // tq),
        in_specs=[
            pl.BlockSpec((None, tq, gw), lambda bi, g, qi: (bi, qi, L1_QD // gw + g)),
            pl.BlockSpec((None, s, gw), lambda bi, g, qi: (bi, 0, L1_KD // gw + g)),
            pl.BlockSpec((None, s, gw), lambda bi, g, qi: (bi, 0, L1_VD // gw + g)),
            pl.BlockSpec((None, hg, 1, s), lambda bi, g, qi: (bi, g, 0, 0)),
            pl.BlockSpec((None, hg, tq, 1), lambda bi, g, qi: (bi, g, qi, 0)),
        ],
        out_specs=pl.BlockSpec((None, tq, gw), lambda bi, g, qi: (bi, qi, g)),
        out_shape=jax.ShapeDtypeStruct((b, s, N_HEADS_D * HEAD_DIM), BF16),
        scratch_shapes=[pltpu.VMEM((hg, tq, LANES), F32) for _ in range(3)],
        compiler_params=pltpu.CompilerParams(
            dimension_semantics=("parallel", "parallel", "arbitrary"), vmem_limit_bytes=VMEM_LIMIT),
        name="fox",
    )(main3, main3, main3, crow, ccol)


def _outproj_kernel(x_ref, ya_ref, yb_ref, gate_ref, w_ref, gf_ref, o_ref, *, na, final_norm):
    g = gate_ref[...].astype(F32)
    silu = g / (1.0 + jnp.exp(-g))
    ya = (ya_ref[...].astype(F32) * silu[:, :na]).astype(BF16)
    yb = (yb_ref[...].astype(F32) * silu[:, na:]).astype(BF16)
    out = x_ref[...] + _dot(ya, w_ref[:na, :]) + _dot(yb, w_ref[na:, :])
    if final_norm:
        ms = jnp.mean(out * out, axis=-1, keepdims=True)
        out = (out * lax.rsqrt(ms + EPS)) * gf_ref[...]
    o_ref[...] = out


def _outproj(x2, ya, yb, main2, w, gf, final_norm, tm):
    m, d = x2.shape
    na, nb = ya.shape[1], yb.shape[1]
    assert na + nb == w.shape[0] and m % tm == 0
    return pl.pallas_call(
        functools.partial(_outproj_kernel, na=na, final_norm=final_norm),
        grid=(m // tm,),
        in_specs=[
            pl.BlockSpec((tm, d), lambda i: (i, 0)),
            pl.BlockSpec((tm, na), lambda i: (i, 0)),
            pl.BlockSpec((tm, nb), lambda i: (i, 0)),
            pl.BlockSpec((tm, na + nb), lambda i: (i, 0)),
            pl.BlockSpec(w.shape, lambda i: (0, 0)),
            pl.BlockSpec((1, d), lambda i: (0, 0)),
        ],
        out_specs=pl.BlockSpec((tm, d), lambda i: (i, 0)),
        out_shape=jax.ShapeDtypeStruct((m, d), F32),
        compiler_params=pltpu.CompilerParams(
            dimension_semantics=("parallel",), vmem_limit_bytes=VMEM_LIMIT),
        name="outproj",
    )(x2, ya, yb, main2, w, gf.reshape(1, d))


def _split_cols(w, sizes):
    offs = np.cumsum(sizes)[:-1].tolist()
    return jnp.split(w, offs, axis=1)


def _layer0_weights(w_in0):
    d = w_in0.shape[0]
    a_q, iq_w, b_w = N_HEADS_A * HEAD_DIM, IDX_HEADS * IDX_DIM, N_HEADS_B * HEAD_DIM
    qa, ka, va, iq, ik, iw, qb, kb, vb, gate = _split_cols(
        w_in0, (a_q, HEAD_DIM, HEAD_DIM, iq_w, IDX_DIM, IDX_HEADS, b_w, b_w, b_w, d))
    z = lambda n: jnp.zeros((d, n), w_in0.dtype)
    iklo = jnp.concatenate([ik, iw, z(LANES - IDX_DIM - IDX_HEADS)], axis=1)
    ikhi = jnp.concatenate([z(LANES - IDX_DIM), ik], axis=1)
    w = jnp.concatenate([gate, iq, qb, qa, ka, va, iklo, ikhi, kb, vb], axis=1)
    assert w.shape[1] == L0_WIDTH
    blocks = ([(E_NONE, E_NONE)] * 8 + [(E_IDX64, E_IDX64)] * 4 + [(E_Q128, E_Q128)] * 3
              + [(E_Q128, E_Q128)] * 5 + [(E_K128, E_NONE)] + [(E_IKLO, E_IDX64)]
              + [(E_K128, E_K128)] * 3 + [(E_NONE, E_NONE)] * 3)
    return w.astype(BF16), blocks, L0_IKLO // MXU_COLS


def _layer1_weights(w_in1):
    d = w_in1.shape[0]
    c_w, d_w = N_HEADS_C * HEAD_DIM, N_HEADS_D * HEAD_DIM
    qc, kc, vc, qd, kd, vd, fl, gate = _split_cols(w_in1, (c_w, c_w, c_w, d_w, d_w, d_w, N_HEADS_D, d))
    flp = jnp.concatenate([fl, jnp.zeros((d, MXU_COLS - N_HEADS_D), w_in1.dtype)], axis=1)
    w = jnp.concatenate([gate, qc, kc, vc, qd, kd, vd, flp], axis=1)
    assert w.shape[1] == L1_WIDTH
    blocks = ([(E_NONE, E_NONE)] * 8 + [(E_QSCALE, E_QSCALE)] * 4 + [(E_NONE, E_NONE)] * 8
              + [(E_QSCALE, E_QSCALE)] * 4 + [(E_NONE, E_NONE)] * 8 + [(E_NONE, E_NONE)])
    return w.astype(BF16), blocks, L1_FL // MXU_COLS


def kernel(x, norm0, w_in0, w_out0, norm1, w_in1, b_f1, w_out1, norm_f):
    b, s, d = x.shape
    m = b * s
    x2 = x.reshape(m, d)
    tm_in = min(512, s)
    tm_out = min(256, s)

    w0, blocks0, small0 = _layer0_weights(w_in0)
    main0, small0_out = _inproj(x2, norm0, w0, _rope_tables(s), blocks0, small0, s, tm_in)
    main0_3 = main0.reshape(b, s, L0_WIDTH)
    y_a = _dsa(main0_3, small0_out.reshape(b, s, MXU_COLS), tq=128, kc=512)
    y_b = _dilated(main0_3)
    x2 = _outproj(x2, y_a.reshape(m, -1), y_b.reshape(m, -1), main0, w_out0.astype(BF16), norm_f,
                  final_norm=False, tm=tm_out)

    w1, blocks1, small1 = _layer1_weights(w_in1)
    main1, small1_out = _inproj(x2, norm1, w1, None, blocks1, small1, s, tm_in)
    main1_3 = main1.reshape(b, s, L1_WIDTH)
    y_c = _stick(main1_3, tq=256, ck=128, hg=4)
    fl = small1_out[:, :N_HEADS_D].reshape(b, s, N_HEADS_D).transpose(0, 2, 1)
    c = _forget_cumsum(fl.reshape(b, N_HEADS_D, s // LANES, LANES),
                       jnp.broadcast_to(b_f1.astype(F32)[:, None, None], (N_HEADS_D, 1, LANES)))
    crow = c.reshape(b, N_HEADS_D, 1, s)
    ccol = c.reshape(b, N_HEADS_D, s, 1)
    y_d = _fox(main1_3, crow, ccol, tq=256, hg=4)
    out = _outproj(x2, y_c.reshape(m, -1), y_d.reshape(m, -1), main1, w_out1.astype(BF16), norm_f,
                   final_norm=True, tm=tm_out)
    return out.reshape(b, s, d)
```

```python
import functools
import math

import numpy as np
import jax
import jax.numpy as jnp
from jax import lax
from jax.experimental import pallas as pl
from jax.experimental.pallas import tpu as pltpu

F32 = jnp.float32
BF16 = jnp.bfloat16
I32 = jnp.int32

HEAD_DIM = 128
N_HEADS_A = 10
N_HEADS_B = 6
N_HEADS_C = 8
N_HEADS_D = 8
IDX_HEADS = 16
IDX_DIM = 64
TOPK_MAX = 256
DILATED = ((128, 1), (512, 4), (2048, 16))
ROPE_THETA = 10000.0
EPS = 1e-6

LANES = 128
MXU_COLS = 256
VMEM_LIMIT = 56 * 1024 * 1024

NEG_MASKED = -1.0e30
NEG_INIT = -0.5e30
INT_MIN = np.int32(-2 ** 31)

L0_GATE, L0_IQ, L0_QB, L0_QA = 0, 2048, 3072, 3840
L0_KA, L0_VA, L0_IKLO, L0_IKHI, L0_KB, L0_VB = 5120, 5248, 5376, 5504, 5632, 6400
L0_WIDTH = 7168
L1_GATE, L1_QC, L1_KC, L1_VC, L1_QD, L1_KD, L1_VD, L1_FL = 0, 2048, 3072, 4096, 5120, 6144, 7168, 8192
L1_WIDTH = 8448

E_NONE, E_Q128, E_K128, E_QSCALE, E_IDX64, E_IKLO = range(6)


def _dot_nt(a, b):
    return lax.dot_general(a, b, (((1,), (1,)), ((), ())), preferred_element_type=F32)


def _dot(a, b):
    return jnp.dot(a, b, preferred_element_type=F32)


def _log_sigmoid(z):
    return jnp.minimum(z, 0.0) - jnp.log(1.0 + jnp.exp(-jnp.abs(z)))


def _any_of(j, values):
    cond = j == values[0]
    for v in values[1:]:
        cond = jnp.logical_or(cond, j == v)
    return cond


def _rope128(t, cos, sin):
    return t * cos + pltpu.roll(t, 64, 1) * sin


def _rope64(t, cos, sin_up, sin_dn):
    return t * cos + pltpu.roll(t, 96, 1) * sin_up + pltpu.roll(t, 32, 1) * sin_dn


def _inproj_kernel(*refs, blocks, small_block, use_rope):
    if use_rope:
        x_ref, g_ref, w_ref, c128, s128, c64, su64, sd64, o_ref, small_ref, h_scr = refs
    else:
        x_ref, g_ref, w_ref, o_ref, small_ref, h_scr = refs
    j = pl.program_id(1)

    @pl.when(j == 0)
    def _():
        x = x_ref[...]
        ms = jnp.mean(x * x, axis=-1, keepdims=True)
        h_scr[...] = ((x * lax.rsqrt(ms + EPS)) * g_ref[...]).astype(BF16)

    acc = _dot(h_scr[...], w_ref[...])
    scale = HEAD_DIM ** -0.5

    def half(a, kind):
        if kind == E_NONE:
            f = a
        elif kind == E_Q128:
            f = _rope128(a * scale, c128[...], s128[...])
        elif kind == E_K128:
            f = _rope128(a, c128[...], s128[...])
        elif kind == E_QSCALE:
            f = a * scale
        elif kind == E_IDX64:
            f = _rope64(a, c64[...], su64[...], sd64[...])
        elif kind == E_IKLO:
            lane = lax.broadcasted_iota(I32, a.shape, 1)
            r = _rope64(a, c64[...], su64[...], sd64[...])
            f = jnp.where(lane < IDX_DIM, r, a)
            return f, jnp.where(lane < IDX_DIM, r, 0.0).astype(BF16)
        else:
            raise ValueError(kind)
        return f, f.astype(BF16)

    groups = {}
    for jj, kinds in enumerate(blocks):
        groups.setdefault((kinds, jj == small_block), []).append(jj)
    for (kinds, is_small), js in groups.items():
        @pl.when(_any_of(j, js))
        def _(kinds=kinds, is_small=is_small):
            f0, b0 = half(acc[:, :LANES], kinds[0])
            f1, b1 = half(acc[:, LANES:], kinds[1])
            o_ref[:, :LANES] = b0
            o_ref[:, LANES:] = b1
            if is_small:
                small_ref[:, :LANES] = f0
                small_ref[:, LANES:] = f1


def _inproj(x2, g, w, tables, blocks, small_block, seq, tm):
    m, d = x2.shape
    n = w.shape[1]
    nblk = n // MXU_COLS
    assert nblk == len(blocks) and m % tm == 0 and seq % tm == 0
    use_rope = tables is not None
    per_seq = seq // tm
    in_specs = [
        pl.BlockSpec((tm, d), lambda i, j: (i, 0)),
        pl.BlockSpec((1, d), lambda i, j: (0, 0)),
        pl.BlockSpec((d, MXU_COLS), lambda i, j: (0, j)),
    ]
    args = [x2, g.reshape(1, d), w]
    if use_rope:
        for t in tables:
            in_specs.append(pl.BlockSpec((tm, LANES), lambda i, j: (i % per_seq, 0)))
            args.append(t)
    return pl.pallas_call(
        functools.partial(_inproj_kernel, blocks=tuple(blocks), small_block=small_block, use_rope=use_rope),
        grid=(m // tm, nblk),
        in_specs=in_specs,
        out_specs=[
            pl.BlockSpec((tm, MXU_COLS), lambda i, j: (i, j)),
            pl.BlockSpec((tm, MXU_COLS), lambda i, j: (i, 0)),
        ],
        out_shape=[
            jax.ShapeDtypeStruct((m, n), BF16),
            jax.ShapeDtypeStruct((m, MXU_COLS), F32),
        ],
        scratch_shapes=[pltpu.VMEM((tm, d), BF16)],
        compiler_params=pltpu.CompilerParams(
            dimension_semantics=("parallel", "arbitrary"), vmem_limit_bytes=VMEM_LIMIT),
        name="inproj",
    )(*args)


def _rope_tables(seq):
    pos = jnp.arange(seq, dtype=F32)[:, None]
    half = HEAD_DIM // 2
    inv = ROPE_THETA ** (-jnp.arange(half, dtype=F32) / half)
    ang = pos * inv[None, :]
    c128 = jnp.concatenate([jnp.cos(ang), jnp.cos(ang)], axis=1)
    s128 = jnp.concatenate([-jnp.sin(ang), jnp.sin(ang)], axis=1)
    half = IDX_DIM // 2
    inv = ROPE_THETA ** (-jnp.arange(half, dtype=F32) / half)
    ang = pos * inv[None, :]
    zero = jnp.zeros_like(ang)
    c64 = jnp.concatenate([jnp.cos(ang)] * 4, axis=1)
    su64 = jnp.concatenate([-jnp.sin(ang), zero] * 2, axis=1)
    sd64 = jnp.concatenate([zero, jnp.sin(ang)] * 2, axis=1)
    return c128, s128, c64, su64, sd64


def _transpose_values(v_ref, vt_scr, hg, seq):
    piece = 256
    eye = (lax.broadcasted_iota(I32, (LANES, LANES), 0) == lax.broadcasted_iota(I32, (LANES, LANES), 1))
    eye = jnp.where(eye, 1.0, 0.0).astype(BF16)

    def body(i, carry):
        start = pl.multiple_of(i * piece, piece)
        for h in range(hg):
            vt = _dot_nt(eye, v_ref[pl.ds(start, piece), h * LANES:(h + 1) * LANES])
            vt_scr[h, :, pl.ds(start, piece)] = vt.astype(BF16)
        return carry

    lax.fori_loop(0, seq // piece, body, 0)


def _dsa_kernel(iq_ref, iwt_ref, iklo_ref, ikhi_ref, qa_ref, ka_ref, va_ref, o_ref,
                key_scr, vt_scr, m_scr, l_scr, acc_scr, *, tq, topk, seq, head_group):
    qi = pl.program_id(1)
    nch = qi + 1
    reps = tq // LANES
    key_i = lax.broadcasted_iota(I32, (tq, tq), 0)
    qry_i = lax.broadcasted_iota(I32, (tq, tq), 1)

    @pl.when(qi == 0)
    def _():
        _transpose_values(va_ref, vt_scr, 1, seq)

    wscale = (IDX_DIM ** -0.5) * (IDX_HEADS ** -0.5)
    w = iwt_ref[...] * wscale

    def score_chunk(c, carry):
        start = pl.multiple_of(c * tq, tq)
        klo = iklo_ref[pl.ds(start, tq), :]
        khi = ikhi_ref[pl.ds(start, tq), :]
        score = jnp.zeros((tq, tq), F32)
        for p in range(IDX_HEADS // 2):
            iq2 = iq_ref[:, p * LANES:(p + 1) * LANES]
            d0 = _dot_nt(klo, iq2)
            d1 = _dot_nt(khi, iq2)
            score = score + w[2 * p:2 * p + 1, :] * jnp.maximum(d0, 0.0) + w[2 * p + 1:2 * p + 2, :] * jnp.maximum(d1, 0.0)
        bits = pltpu.bitcast(score, I32)
        okey = jnp.where(bits < 0, bits ^ np.int32(0x7FFFFFFF), bits)
        okey = jnp.where(key_i + (c - qi) * tq <= qry_i, okey, INT_MIN)
        key_scr[pl.ds(start, tq), :] = okey
        return carry

    lax.fori_loop(0, nch, score_chunk, 0)

    def bit_step(it, t_u):
        bit = jnp.left_shift(np.int32(1), 31 - it)
        cand_u = t_u | bit
        cand_s = cand_u ^ INT_MIN

        def count_chunk(c, cnt):
            start = pl.multiple_of(c * tq, tq)
            k = key_scr[pl.ds(start, tq), :]
            hit = jnp.where(k >= cand_s, 1.0, 0.0)
            return cnt + jnp.sum(hit.reshape(tq // 8, 8, tq), axis=0)

        cnt = lax.fori_loop(0, nch, count_chunk, jnp.zeros((8, tq), F32))
        total = jnp.sum(cnt, axis=0, keepdims=True)
        return jnp.where(total >= float(topk), cand_u, t_u)

    t_u = lax.fori_loop(0, 32, bit_step, jnp.zeros((1, tq), I32))
    thr = jnp.maximum(t_u ^ INT_MIN, INT_MIN + np.int32(1))

    m_scr[...] = jnp.full(m_scr.shape, NEG_INIT, F32)
    l_scr[...] = jnp.zeros(l_scr.shape, F32)
    acc_scr[...] = jnp.zeros(acc_scr.shape, F32)

    def attn_chunk(c, carry):
        start = pl.multiple_of(c * tq, tq)
        kk = ka_ref[pl.ds(start, tq), :]
        vt = vt_scr[0, :, pl.ds(start, tq)]
        sel = key_scr[pl.ds(start, tq), :] >= thr
        for g0 in range(0, N_HEADS_A, head_group):
            group = range(g0, min(g0 + head_group, N_HEADS_A))
            scores = {h: _dot_nt(kk, qa_ref[:, h * LANES:(h + 1) * LANES]) for h in group}
            probs, alphas = {}, {}
            for h in group:
                sm = jnp.where(sel, scores[h], NEG_MASKED)
                m_old = m_scr[h, 0:1, :]
                m_new = jnp.maximum(m_old, jnp.max(sm, axis=0, keepdims=True))
                alphas[h] = jnp.exp(m_old - m_new)
                p = jnp.exp(sm - m_new)
                l_scr[h] = jnp.broadcast_to(alphas[h] * l_scr[h, 0:1, :] + jnp.sum(p, axis=0, keepdims=True), (8, tq))
                m_scr[h] = jnp.broadcast_to(m_new, (8, tq))
                probs[h] = p.astype(BF16)
            for h in group:
                acc_scr[h] = alphas[h] * acc_scr[h] + _dot(vt, probs[h])
        return carry

    lax.fori_loop(0, nch, attn_chunk, 0)
    for h in range(N_HEADS_A):
        o_ref[:, h * LANES:(h + 1) * LANES] = (acc_scr[h] / l_scr[h, 0:1, :]).T.astype(BF16)


def _dsa(main3, iwt, tq, head_group):
    b, s, _ = main3.shape
    topk = min(TOPK_MAX, s // 4)
    a_q = N_HEADS_A * HEAD_DIM
    iq_w = IDX_HEADS * IDX_DIM
    assert s % tq == 0 and s % 256 == 0

    def slot(off):
        return lambda bi, qi: (bi, 0, off // LANES)

    return pl.pallas_call(
        functools.partial(_dsa_kernel, tq=tq, topk=topk, seq=s, head_group=head_group),
        grid=(b, s // tq),
        in_specs=[
            pl.BlockSpec((None, tq, iq_w), lambda bi, qi: (bi, qi, L0_IQ // iq_w)),
            pl.BlockSpec((None, IDX_HEADS, tq), lambda bi, qi: (bi, 0, qi)),
            pl.BlockSpec((None, s, LANES), slot(L0_IKLO)),
            pl.BlockSpec((None, s, LANES), slot(L0_IKHI)),
            pl.BlockSpec((None, tq, a_q), lambda bi, qi: (bi, qi, L0_QA // a_q)),
            pl.BlockSpec((None, s, LANES), slot(L0_KA)),
            pl.BlockSpec((None, s, LANES), slot(L0_VA)),
        ],
        out_specs=pl.BlockSpec((None, tq, a_q), lambda bi, qi: (bi, qi, 0)),
        out_shape=jax.ShapeDtypeStruct((b, s, a_q), BF16),
        scratch_shapes=[
            pltpu.VMEM((s, tq), I32),
            pltpu.VMEM((1, LANES, s), BF16),
            pltpu.VMEM((N_HEADS_A, 8, tq), F32),
            pltpu.VMEM((N_HEADS_A, 8, tq), F32),
            pltpu.VMEM((N_HEADS_A, LANES, tq), F32),
        ],
        compiler_params=pltpu.CompilerParams(
            dimension_semantics=("parallel", "arbitrary"), vmem_limit_bytes=VMEM_LIMIT),
        name="dsa",
    )(main3, iwt, main3, main3, main3, main3, main3)


def _dilated_kernel(q_ref, k_ref, v_ref, o_ref, qf, kf, vf, o0, o1, o2, e0, e1, e2, *, seq, blk):
    qf[...] = q_ref[...].astype(F32)
    kf[...] = k_ref[...].astype(F32)
    vf[...] = v_ref[...].astype(F32)
    o_scr = (o0, o1, o2)
    e_scr = (e0, e1, e2)
    row = lax.broadcasted_iota(I32, (blk, blk), 0)
    col = lax.broadcasted_iota(I32, (blk, blk), 1)

    for pat, (window, dil) in enumerate(DILATED):
        steps = window // dil
        length = seq // dil
        nb = length // blk
        cur_ok = jnp.logical_and(row - col >= 0, row - col <= steps)
        prev_band = jnp.logical_and(row + blk - col >= 0, row + blk - col <= steps)

        def rows(ref, n, r, dil=dil):
            start = n * (blk * dil) + r
            if dil == 1:
                return ref[pl.ds(start, blk), :]
            return ref[pl.ds(start, blk, stride=dil), :]

        def step(idx, carry, pat=pat, dil=dil, nb=nb, cur_ok=cur_ok, prev_band=prev_band, rows=rows):
            r = idx // nb
            n = idx % nb
            n_prev = jnp.maximum(n - 1, 0)
            q = rows(qf, n, r).astype(BF16)
            k_cur = rows(kf, n, r).astype(BF16)
            k_prev = rows(kf, n_prev, r).astype(BF16)
            v_cur = rows(vf, n, r).astype(BF16)
            v_prev = rows(vf, n_prev, r).astype(BF16)
            s_cur = jnp.where(cur_ok, _dot_nt(q, k_cur), NEG_MASKED)
            prev_ok = jnp.logical_and(prev_band, n > 0)
            s_prev = jnp.where(prev_ok, _dot_nt(q, k_prev), NEG_MASKED)
            m = jnp.max(jnp.maximum(s_cur, s_prev), axis=1, keepdims=True)
            p_cur = jnp.exp(s_cur - m)
            p_prev = jnp.exp(s_prev - m)
            l = jnp.sum(p_cur + p_prev, axis=1, keepdims=True)
            o = (_dot(p_cur.astype(BF16), v_cur) + _dot(p_prev.astype(BF16), v_prev)) / l
            lse = jnp.broadcast_to(m + jnp.log(l), (blk, LANES))
            start = n * (blk * dil) + r
            if dil == 1:
                o_scr[pat][pl.ds(start, blk), :] = o
                e_scr[pat][pl.ds(start, blk), :] = lse
            else:
                o_scr[pat][pl.ds(start, blk, stride=dil), :] = o
                e_scr[pat][pl.ds(start, blk, stride=dil), :] = lse
            return carry

        lax.fori_loop(0, dil * nb, step, 0)

    rows_per = 512
    def merge(i, carry):
        sl = pl.ds(pl.multiple_of(i * rows_per, rows_per), rows_per)
        es = [e[sl, :] for e in e_scr]
        mx = jnp.maximum(jnp.maximum(es[0], es[1]), es[2])
        ws = [jnp.exp(e - mx) for e in es]
        num = ws[0] * o_scr[0][sl, :] + ws[1] * o_scr[1][sl, :] + ws[2] * o_scr[2][sl, :]
        o_ref[sl, :] = (num / (ws[0] + ws[1] + ws[2])).astype(BF16)
        return carry

    lax.fori_loop(0, seq // rows_per, merge, 0)


def _dilated(main3):
    b, s, _ = main3.shape
    blk = 128
    max_dil = max(d for _, d in DILATED)
    assert s % (blk * max_dil) == 0 and s % 512 == 0

    def slot(off):
        return lambda bi, h: (bi, 0, off // LANES + h)

    return pl.pallas_call(
        functools.partial(_dilated_kernel, seq=s, blk=blk),
        grid=(b, N_HEADS_B),
        in_specs=[
            pl.BlockSpec((None, s, LANES), slot(L0_QB)),
            pl.BlockSpec((None, s, LANES), slot(L0_KB)),
            pl.BlockSpec((None, s, LANES), slot(L0_VB)),
        ],
        out_specs=pl.BlockSpec((None, s, LANES), lambda bi, h: (bi, 0, h)),
        out_shape=jax.ShapeDtypeStruct((b, s, N_HEADS_B * HEAD_DIM), BF16),
        scratch_shapes=[pltpu.VMEM((s, LANES), F32) for _ in range(9)],
        compiler_params=pltpu.CompilerParams(
            dimension_semantics=("parallel", "parallel"), vmem_limit_bytes=VMEM_LIMIT),
        name="dilated",
    )(main3, main3, main3)


def _stick_kernel(q_ref, k_ref, v_ref, o_ref, vt_scr, run_scr, acc_scr, z_scr, *, tq, ck, hg, seq):
    qi = pl.program_id(2)
    nch = ((qi + 1) * tq) // ck
    n_diag = tq // ck

    @pl.when(qi == 0)
    def _():
        _transpose_values(v_ref, vt_scr, hg, seq)

    rr = lax.broadcasted_iota(I32, (ck + 8, 2 * ck), 0)
    cc = lax.broadcasted_iota(I32, (ck + 8, 2 * ck), 1) % ck
    later = jnp.where(jnp.logical_or(cc >= rr, rr >= ck), 1.0, 0.0).astype(BF16)
    dsl = (lax.broadcasted_iota(I32, (ck, tq), 0) - lax.broadcasted_iota(I32, (ck, tq), 1))
    heads = range(hg)
    hs = [slice(h * LANES, (h + 1) * LANES) for h in heads]

    run_scr[...] = jnp.zeros(run_scr.shape, F32)
    acc_scr[...] = jnp.zeros(acc_scr.shape, F32)

    def logits(c):
        start = pl.multiple_of(jnp.maximum(c, 0) * ck, ck)
        return [_dot_nt(k_ref[pl.ds(start, ck), hs[h]], q_ref[:, hs[h]]) for h in heads]

    def chunk(c, zs, masked):
        start = pl.multiple_of(c * ck, ck)
        before = dsl < (qi * tq - c * ck)
        stacked = []
        for h in heads:
            z = zs[h]
            sp = jnp.maximum(z, 0.0) + jnp.log(1.0 + jnp.exp(-jnp.abs(z)))
            if masked:
                sp = jnp.where(before, sp, 0.0)
            hi = pltpu.bitcast(pltpu.bitcast(sp, I32) & np.int32(-65536), F32)
            stacked.append(jnp.concatenate([hi.astype(BF16), (sp - hi).astype(BF16)], axis=0))
        sums = [_dot(later, stacked[h]) for h in heads]
        ws = []
        for h in heads:
            run = run_scr[h]
            w = jnp.exp(zs[h] - (sums[h][:ck] + run[0:1, :]))
            if masked:
                w = jnp.where(before, w, 0.0)
            run_scr[h] = run + sums[h][ck:]
            ws.append(w.astype(BF16))
        for h in heads:
            acc_scr[h] += _dot(vt_scr[h, :, pl.ds(start, ck)], ws[h])

    first = nch - 1
    zs = logits(first)
    for d in range(n_diag):
        nxt = logits(first - d - 1)
        chunk(first - d, zs, True)
        zs = nxt
    for h in heads:
        z_scr[0, h] = zs[h]

    def off_diagonal(it, carry):
        c = first - n_diag - it
        slot = it % 2
        zs = [z_scr[slot, h] for h in heads]
        nxt = logits(c - 1)
        chunk(c, zs, False)
        for h in heads:
            z_scr[1 - slot, h] = nxt[h]
        return carry

    lax.fori_loop(0, nch - n_diag, off_diagonal, 0)
    for h in heads:
        o_ref[:, hs[h]] = acc_scr[h].T.astype(BF16)


def _stick(main3, tq, ck, hg):
    b, s, _ = main3.shape
    gw = hg * HEAD_DIM
    assert N_HEADS_C % hg == 0 and tq % ck == 0 and s % tq == 0 and s % 256 == 0
    return pl.pallas_call(
        functools.partial(_stick_kernel, tq=tq, ck=ck, hg=hg, seq=s),
        grid=(b, N_HEADS_C // hg, s // tq),
        in_specs=[
            pl.BlockSpec((None, tq, gw), lambda bi, g, qi: (bi, qi, L1_QC // gw + g)),
            pl.BlockSpec((None, s, gw), lambda bi, g, qi: (bi, 0, L1_KC // gw + g)),
            pl.BlockSpec((None, s, gw), lambda bi, g, qi: (bi, 0, L1_VC // gw + g)),
        ],
        out_specs=pl.BlockSpec((None, tq, gw), lambda bi, g, qi: (bi, qi, g)),
        out_shape=jax.ShapeDtypeStruct((b, s, N_HEADS_C * HEAD_DIM), BF16),
        scratch_shapes=[
            pltpu.VMEM((hg, LANES, s), BF16),
            pltpu.VMEM((hg, 8, tq), F32),
            pltpu.VMEM((hg, LANES, tq), F32),
            pltpu.VMEM((2, hg, ck, tq), F32),
        ],
        compiler_params=pltpu.CompilerParams(
            dimension_semantics=("parallel", "parallel", "arbitrary"), vmem_limit_bytes=VMEM_LIMIT),
        name="stick",
    )(main3, main3, main3)


def _split3(x):
    x1 = x.astype(BF16)
    r1 = x - x1.astype(F32)
    x2 = r1.astype(BF16)
    x3 = (r1 - x2.astype(F32)).astype(BF16)
    return x1, x2, x3


def _forget_cumsum_kernel(fl_ref, b_ref, c_ref, *, heads, rows):
    x = fl_ref[...] + b_ref[...]
    lf = _log_sigmoid(x).reshape(heads * rows, LANES)
    jj = lax.broadcasted_iota(I32, (LANES, 2 * LANES), 0)
    ss = lax.broadcasted_iota(I32, (LANES, 2 * LANES), 1)
    upper = jnp.where(jnp.logical_or(jj <= ss, ss >= LANES), 1.0, 0.0).astype(BF16)
    both = sum(_dot(part, upper) for part in _split3(lf))
    within = both[:, :LANES]
    total = both[:, LANES:]
    n = heads * rows
    aa = lax.broadcasted_iota(I32, (n, n), 0)
    bb = lax.broadcasted_iota(I32, (n, n), 1)
    same_head = (aa // rows) == (bb // rows)
    lower = jnp.where(jnp.logical_and(same_head, bb < aa), 1.0, 0.0).astype(BF16)
    offs = sum(_dot(lower, part) for part in _split3(total))
    c_ref[...] = (within + offs).reshape(heads, rows, LANES)


def _forget_cumsum(fl4, bf3):
    b, heads, rows, _ = fl4.shape
    return pl.pallas_call(
        functools.partial(_forget_cumsum_kernel, heads=heads, rows=rows),
        grid=(b,),
        in_specs=[
            pl.BlockSpec((None, heads, rows, LANES), lambda bi: (bi, 0, 0, 0)),
            pl.BlockSpec((heads, 1, LANES), lambda bi: (0, 0, 0)),
        ],
        out_specs=pl.BlockSpec((None, heads, rows, LANES), lambda bi: (bi, 0, 0, 0)),
        out_shape=jax.ShapeDtypeStruct(fl4.shape, F32),
        compiler_params=pltpu.CompilerParams(dimension_semantics=("parallel",)),
        name="forget_cumsum",
    )(fl4, bf3)


def _fox_kernel(q_ref, k_ref, v_ref, crow_ref, o_ref, vt_scr, cb_scr, m_scr, l_scr, acc_scr, s_scr,
                *, tq, hg, seq):
    qi = pl.program_id(2)
    heads = range(hg)
    hs = [slice(h * LANES, (h + 1) * LANES) for h in heads]
    reps = tq // LANES

    @pl.when(qi == 0)
    def _():
        _transpose_values(v_ref, vt_scr, hg, seq)

        def body(i, carry):
            start = pl.multiple_of(i * LANES, LANES)
            for h in heads:
                row = crow_ref[h, :, pl.ds(start, LANES)]
                cb_scr[h, pl.ds(start, LANES), :] = jnp.broadcast_to(row, (LANES, LANES)).T
            return carry

        lax.fori_loop(0, seq // LANES, body, 0)

    m_scr[...] = jnp.full(m_scr.shape, NEG_INIT, F32)
    l_scr[...] = jnp.zeros(l_scr.shape, F32)
    acc_scr[...] = jnp.zeros(acc_scr.shape, F32)
    c_t = [crow_ref[h, :, pl.ds(pl.multiple_of(qi * tq, tq), tq)] for h in heads]

    def logits(c):
        start = pl.multiple_of(c * tq, tq)
        return [_dot_nt(k_ref[pl.ds(start, tq), hs[h]], q_ref[:, hs[h]]) for h in heads]

    def chunk(c, scores, diagonal):
        start = pl.multiple_of(c * tq, tq)
        probs, alphas = [], []
        for h in heads:
            cb = cb_scr[h, pl.ds(start, tq), :]
            u = scores[h] - (jnp.concatenate([cb] * reps, axis=1) if reps > 1 else cb)
            if diagonal:
                key = lax.broadcasted_iota(I32, (tq, tq), 0)
                qry = lax.broadcasted_iota(I32, (tq, tq), 1)
                u = jnp.where(key <= qry, u, NEG_MASKED)
            m_old = m_scr[h, 0:1, :]
            m_new = jnp.maximum(m_old, jnp.max(u, axis=0, keepdims=True) + c_t[h])
            alpha = jnp.exp(m_old - m_new)
            p = jnp.exp(u - (m_new - c_t[h]))
            l_scr[h] = jnp.broadcast_to(alpha * l_scr[h, 0:1, :] + jnp.sum(p, axis=0, keepdims=True), (8, tq))
            m_scr[h] = jnp.broadcast_to(m_new, (8, tq))
            probs.append(p.astype(BF16))
            alphas.append(alpha)
        for h in heads:
            acc_scr[h] = alphas[h] * acc_scr[h] + _dot(vt_scr[h, :, pl.ds(start, tq)], probs[h])

    first = logits(0)
    for h in heads:
        s_scr[0, h] = first[h]

    def off_diagonal(c, carry):
        slot = c % 2
        scores = [s_scr[slot, h] for h in heads]
        nxt = logits(c + 1)
        chunk(c, scores, False)
        for h in heads:
            s_scr[1 - slot, h] = nxt[h]
        return carry

    lax.fori_loop(0, qi, off_diagonal, 0)
    chunk(qi, [s_scr[qi % 2, h] for h in heads], True)
    for h in heads:
        o_ref[:, hs[h]] = (acc_scr[h] / l_scr[h, 0:1, :]).T.astype(BF16)


def _fox(main3, crow, tq, hg):
    b, s, _ = main3.shape
    gw = hg * HEAD_DIM
    assert N_HEADS_D % hg == 0 and s % tq == 0 and s % 256 == 0
    return pl.pallas_call(
        functools.partial(_fox_kernel, tq=tq, hg=hg, seq=s),
        grid=(b, N_HEADS_D // hg, s // tq),
        in_specs=[
            pl.BlockSpec((None, tq, gw), lambda bi, g, qi: (bi, qi, L1_QD // gw + g)),
            pl.BlockSpec((None, s, gw), lambda bi, g, qi: (bi, 0, L1_KD // gw + g)),
            pl.BlockSpec((None, s, gw), lambda bi, g, qi: (bi, 0, L1_VD // gw + g)),
            pl.BlockSpec((None, hg, 1, s), lambda bi, g, qi: (bi, g, 0, 0)),
        ],
        out_specs=pl.BlockSpec((None, tq, gw), lambda bi, g, qi: (bi, qi, g)),
        out_shape=jax.ShapeDtypeStruct((b, s, N_HEADS_D * HEAD_DIM), BF16),
        scratch_shapes=[
            pltpu.VMEM((hg, LANES, s), BF16),
            pltpu.VMEM((hg, s, LANES), F32),
            pltpu.VMEM((hg, 8, tq), F32),
            pltpu.VMEM((hg, 8, tq), F32),
            pltpu.VMEM((hg, LANES, tq), F32),
            pltpu.VMEM((2, hg, tq, tq), F32),
        ],
        compiler_params=pltpu.CompilerParams(
            dimension_semantics=("parallel", "parallel", "arbitrary"), vmem_limit_bytes=VMEM_LIMIT),
        name="fox",
    )(main3, main3, main3, crow)


def _outproj_kernel(x_ref, ya_ref, yb_ref, gate_ref, w_ref, gf_ref, o_ref, *, na, final_norm):
    g = gate_ref[...].astype(F32)
    silu = g / (1.0 + jnp.exp(-g))
    ya = (ya_ref[...].astype(F32) * silu[:, :na]).astype(BF16)
    yb = (yb_ref[...].astype(F32) * silu[:, na:]).astype(BF16)
    out = x_ref[...] + _dot(ya, w_ref[:na, :]) + _dot(yb, w_ref[na:, :])
    if final_norm:
        ms = jnp.mean(out * out, axis=-1, keepdims=True)
        out = (out * lax.rsqrt(ms + EPS)) * gf_ref[...]
    o_ref[...] = out


def _outproj(x2, ya, yb, main2, w, gf, final_norm, tm):
    m, d = x2.shape
    na, nb = ya.shape[1], yb.shape[1]
    assert na + nb == w.shape[0] and m % tm == 0
    return pl.pallas_call(
        functools.partial(_outproj_kernel, na=na, final_norm=final_norm),
        grid=(m // tm,),
        in_specs=[
            pl.BlockSpec((tm, d), lambda i: (i, 0)),
            pl.BlockSpec((tm, na), lambda i: (i, 0)),
            pl.BlockSpec((tm, nb), lambda i: (i, 0)),
            pl.BlockSpec((tm, na + nb), lambda i: (i, 0)),
            pl.BlockSpec(w.shape, lambda i: (0, 0)),
            pl.BlockSpec((1, d), lambda i: (0, 0)),
        ],
        out_specs=pl.BlockSpec((tm, d), lambda i: (i, 0)),
        out_shape=jax.ShapeDtypeStruct((m, d), F32),
        compiler_params=pltpu.CompilerParams(
            dimension_semantics=("parallel",), vmem_limit_bytes=VMEM_LIMIT),
        name="outproj",
    )(x2, ya, yb, main2, w, gf.reshape(1, d))


def _split_cols(w, sizes):
    offs = np.cumsum(sizes)[:-1].tolist()
    return jnp.split(w, offs, axis=1)


def _layer0_weights(w_in0):
    d = w_in0.shape[0]
    a_q, iq_w, b_w = N_HEADS_A * HEAD_DIM, IDX_HEADS * IDX_DIM, N_HEADS_B * HEAD_DIM
    qa, ka, va, iq, ik, iw, qb, kb, vb, gate = _split_cols(
        w_in0, (a_q, HEAD_DIM, HEAD_DIM, iq_w, IDX_DIM, IDX_HEADS, b_w, b_w, b_w, d))
    z = lambda n: jnp.zeros((d, n), w_in0.dtype)
    iklo = jnp.concatenate([ik, iw, z(LANES - IDX_DIM - IDX_HEADS)], axis=1)
    ikhi = jnp.concatenate([z(LANES - IDX_DIM), ik], axis=1)
    w = jnp.concatenate([gate, iq, qb, qa, ka, va, iklo, ikhi, kb, vb], axis=1)
    assert w.shape[1] == L0_WIDTH
    blocks = ([(E_NONE, E_NONE)] * 8 + [(E_IDX64, E_IDX64)] * 4 + [(E_Q128, E_Q128)] * 3
              + [(E_Q128, E_Q128)] * 5 + [(E_K128, E_NONE)] + [(E_IKLO, E_IDX64)]
              + [(E_K128, E_K128)] * 3 + [(E_NONE, E_NONE)] * 3)
    return w.astype(BF16), blocks, L0_IKLO // MXU_COLS


def _layer1_weights(w_in1):
    d = w_in1.shape[0]
    c_w, d_w = N_HEADS_C * HEAD_DIM, N_HEADS_D * HEAD_DIM
    qc, kc, vc, qd, kd, vd, fl, gate = _split_cols(w_in1, (c_w, c_w, c_w, d_w, d_w, d_w, N_HEADS_D, d))
    flp = jnp.concatenate([fl, jnp.zeros((d, MXU_COLS - N_HEADS_D), w_in1.dtype)], axis=1)
    w = jnp.concatenate([gate, qc, kc, vc, qd, kd, vd, flp], axis=1)
    assert w.shape[1] == L1_WIDTH
    blocks = ([(E_NONE, E_NONE)] * 8 + [(E_QSCALE, E_QSCALE)] * 4 + [(E_NONE, E_NONE)] * 8
              + [(E_QSCALE, E_QSCALE)] * 4 + [(E_NONE, E_NONE)] * 8 + [(E_NONE, E_NONE)])
    return w.astype(BF16), blocks, L1_FL // MXU_COLS


def kernel(x, norm0, w_in0, w_out0, norm1, w_in1, b_f1, w_out1, norm_f):
    b, s, d = x.shape
    m = b * s
    x2 = x.reshape(m, d)
    tm_in = min(512, s)
    tm_out = min(256, s)

    w0, blocks0, small0 = _layer0_weights(w_in0)
    main0, small0_out = _inproj(x2, norm0, w0, _rope_tables(s), blocks0, small0, s, tm_in)
    main0_3 = main0.reshape(b, s, L0_WIDTH)
    iwt = small0_out[:, IDX_DIM:IDX_DIM + IDX_HEADS].reshape(b, s, IDX_HEADS).transpose(0, 2, 1)
    y_a = _dsa(main0_3, iwt, tq=256, head_group=5)
    y_b = _dilated(main0_3)
    x2 = _outproj(x2, y_a.reshape(m, -1), y_b.reshape(m, -1), main0, w_out0.astype(BF16), norm_f,
                  final_norm=False, tm=tm_out)

    w1, blocks1, small1 = _layer1_weights(w_in1)
    main1, small1_out = _inproj(x2, norm1, w1, None, blocks1, small1, s, tm_in)
    main1_3 = main1.reshape(b, s, L1_WIDTH)
    y_c = _stick(main1_3, tq=256, ck=128, hg=8)
    fl = small1_out[:, :N_HEADS_D].reshape(b, s, N_HEADS_D).transpose(0, 2, 1)
    c = _forget_cumsum(fl.reshape(b, N_HEADS_D, s // LANES, LANES),
                       jnp.broadcast_to(b_f1.astype(F32)[:, None, None], (N_HEADS_D, 1, LANES)))
    crow = c.reshape(b, N_HEADS_D, 1, s)
    y_d = _fox(main1_3, crow, tq=256, hg=4)
    out = _outproj(x2, y_c.reshape(m, -1), y_d.reshape(m, -1), main1, w_out1.astype(BF16), norm_f,
                   final_norm=True, tm=tm_out)
    return out.reshape(b, s, d)
```

```python
import functools
import math

import numpy as np
import jax
import jax.numpy as jnp
from jax import lax
from jax.experimental import pallas as pl
from jax.experimental.pallas import tpu as pltpu

F32 = jnp.float32
BF16 = jnp.bfloat16
I32 = jnp.int32

HEAD_DIM = 128
N_HEADS_A = 10
N_HEADS_B = 6
N_HEADS_C = 8
N_HEADS_D = 8
IDX_HEADS = 16
IDX_DIM = 64
TOPK_MAX = 256
DILATED = ((128, 1), (512, 4), (2048, 16))
ROPE_THETA = 10000.0
EPS = 1e-6

LANES = 128
MXU_COLS = 256
VMEM_LIMIT = 56 * 1024 * 1024

NEG_MASKED = -1.0e30
NEG_INIT = -0.5e30
INT_MIN = np.int32(-2 ** 31)

L0_GATE, L0_IQ, L0_QB, L0_QA = 0, 2048, 3072, 3840
L0_KA, L0_VA, L0_IKLO, L0_IKHI, L0_KB, L0_VB = 5120, 5248, 5376, 5504, 5632, 6400
L0_WIDTH = 7168
L1_GATE, L1_QC, L1_KC, L1_VC, L1_QD, L1_KD, L1_VD, L1_FL = 0, 2048, 3072, 4096, 5120, 6144, 7168, 8192
L1_WIDTH = 8448

E_NONE, E_Q128, E_K128, E_QSCALE, E_IDX64, E_IKLO = range(6)


def _dot_nt(a, b):
    return lax.dot_general(a, b, (((1,), (1,)), ((), ())), preferred_element_type=F32)


def _dot(a, b):
    return jnp.dot(a, b, preferred_element_type=F32)


def _log_sigmoid(z):
    return jnp.minimum(z, 0.0) - jnp.log(1.0 + jnp.exp(-jnp.abs(z)))


def _rope128(t, cos, sin):
    return t * cos + pltpu.roll(t, 64, 1) * sin


def _rope64(t, cos, sin_up, sin_dn):
    return t * cos + pltpu.roll(t, 96, 1) * sin_up + pltpu.roll(t, 32, 1) * sin_dn


def _inproj_kernel(*refs, blocks, small_block, use_rope):
    if use_rope:
        x_ref, g_ref, w_ref, c128, s128, c64, su64, sd64, o_ref, small_ref, h_scr = refs
    else:
        x_ref, g_ref, w_ref, o_ref, small_ref, h_scr = refs
    x = x_ref[...]
    ms = jnp.mean(x * x, axis=-1, keepdims=True)
    h_scr[...] = ((x * lax.rsqrt(ms + EPS)) * g_ref[...]).astype(BF16)
    scale = HEAD_DIM ** -0.5

    def half(a, kind):
        if kind == E_NONE:
            f = a
        elif kind == E_Q128:
            f = _rope128(a * scale, c128[...], s128[...])
        elif kind == E_K128:
            f = _rope128(a, c128[...], s128[...])
        elif kind == E_QSCALE:
            f = a * scale
        elif kind == E_IDX64:
            f = _rope64(a, c64[...], su64[...], sd64[...])
        elif kind == E_IKLO:
            lane = lax.broadcasted_iota(I32, a.shape, 1)
            r = _rope64(a, c64[...], su64[...], sd64[...])
            f = jnp.where(lane < IDX_DIM, r, a)
            return f, jnp.where(lane < IDX_DIM, r, 0.0).astype(BF16)
        else:
            raise ValueError(kind)
        return f, f.astype(BF16)

    for jj, kinds in enumerate(blocks):
        c0 = jj * MXU_COLS
        acc = _dot(h_scr[...], w_ref[:, c0:c0 + MXU_COLS])
        f0, b0 = half(acc[:, :LANES], kinds[0])
        f1, b1 = half(acc[:, LANES:], kinds[1])
        o_ref[:, c0:c0 + LANES] = b0
        o_ref[:, c0 + LANES:c0 + MXU_COLS] = b1
        if jj == small_block:
            small_ref[:, :LANES] = f0
            small_ref[:, LANES:] = f1


def _inproj(x2, g, w, tables, blocks, small_block, seq, tm):
    m, d = x2.shape
    n = w.shape[1]
    assert n == len(blocks) * MXU_COLS and m % tm == 0 and seq % tm == 0
    use_rope = tables is not None
    per_seq = seq // tm
    in_specs = [
        pl.BlockSpec((tm, d), lambda i: (i, 0)),
        pl.BlockSpec((1, d), lambda i: (0, 0)),
        pl.BlockSpec((d, n), lambda i: (0, 0), pipeline_mode=pl.Buffered(1)),
    ]
    args = [x2, g.reshape(1, d), w]
    if use_rope:
        for t in tables:
            in_specs.append(pl.BlockSpec((tm, LANES), lambda i: (i % per_seq, 0)))
            args.append(t)
    return pl.pallas_call(
        functools.partial(_inproj_kernel, blocks=tuple(blocks), small_block=small_block, use_rope=use_rope),
        grid=(m // tm,),
        in_specs=in_specs,
        out_specs=[
            pl.BlockSpec((tm, n), lambda i: (i, 0)),
            pl.BlockSpec((tm, MXU_COLS), lambda i: (i, 0)),
        ],
        out_shape=[
            jax.ShapeDtypeStruct((m, n), BF16),
            jax.ShapeDtypeStruct((m, MXU_COLS), F32),
        ],
        scratch_shapes=[pltpu.VMEM((tm, d), BF16)],
        compiler_params=pltpu.CompilerParams(
            dimension_semantics=("parallel",), vmem_limit_bytes=VMEM_LIMIT),
        name="inproj",
    )(*args)


def _rope_tables(seq):
    pos = jnp.arange(seq, dtype=F32)[:, None]
    half = HEAD_DIM // 2
    inv = ROPE_THETA ** (-jnp.arange(half, dtype=F32) / half)
    ang = pos * inv[None, :]
    c128 = jnp.concatenate([jnp.cos(ang), jnp.cos(ang)], axis=1)
    s128 = jnp.concatenate([-jnp.sin(ang), jnp.sin(ang)], axis=1)
    half = IDX_DIM // 2
    inv = ROPE_THETA ** (-jnp.arange(half, dtype=F32) / half)
    ang = pos * inv[None, :]
    zero = jnp.zeros_like(ang)
    c64 = jnp.concatenate([jnp.cos(ang)] * 4, axis=1)
    su64 = jnp.concatenate([-jnp.sin(ang), zero] * 2, axis=1)
    sd64 = jnp.concatenate([zero, jnp.sin(ang)] * 2, axis=1)
    return c128, s128, c64, su64, sd64


def _transpose_values(v_ref, vt_scr, hg, seq):
    piece = 256
    eye = (lax.broadcasted_iota(I32, (LANES, LANES), 0) == lax.broadcasted_iota(I32, (LANES, LANES), 1))
    eye = jnp.where(eye, 1.0, 0.0).astype(BF16)

    def body(i, carry):
        start = pl.multiple_of(i * piece, piece)
        for h in range(hg):
            vt = _dot_nt(eye, v_ref[pl.ds(start, piece), h * LANES:(h + 1) * LANES])
            vt_scr[h, :, pl.ds(start, piece)] = vt.astype(BF16)
        return carry

    lax.fori_loop(0, seq // piece, body, 0)


def _dsa_kernel(iq_ref, iwt_ref, iklo_ref, ikhi_ref, qa_ref, ka_ref, va_ref, o_ref,
                key_scr, vt_scr, m_scr, l_scr, acc_scr, *, tq, topk, seq, head_group):
    qi = pl.program_id(1)
    nch = qi + 1
    reps = tq // LANES
    key_i = lax.broadcasted_iota(I32, (tq, tq), 0)
    qry_i = lax.broadcasted_iota(I32, (tq, tq), 1)

    @pl.when(qi == 0)
    def _():
        _transpose_values(va_ref, vt_scr, 1, seq)

    wscale = (IDX_DIM ** -0.5) * (IDX_HEADS ** -0.5)
    w = iwt_ref[...] * wscale

    def score_chunk(c, carry):
        start = pl.multiple_of(c * tq, tq)
        klo = iklo_ref[pl.ds(start, tq), :]
        khi = ikhi_ref[pl.ds(start, tq), :]
        score = jnp.zeros((tq, tq), F32)
        for p in range(IDX_HEADS // 2):
            iq2 = iq_ref[:, p * LANES:(p + 1) * LANES]
            d0 = _dot_nt(klo, iq2)
            d1 = _dot_nt(khi, iq2)
            score = score + w[2 * p:2 * p + 1, :] * jnp.maximum(d0, 0.0) + w[2 * p + 1:2 * p + 2, :] * jnp.maximum(d1, 0.0)
        bits = pltpu.bitcast(score, I32)
        okey = jnp.where(bits < 0, bits ^ np.int32(0x7FFFFFFF), bits)
        okey = jnp.where(key_i + (c - qi) * tq <= qry_i, okey, INT_MIN)
        key_scr[pl.ds(start, tq), :] = okey
        return carry

    lax.fori_loop(0, nch, score_chunk, 0)
    key_scr[pl.ds(pl.multiple_of(nch * tq, tq), tq), :] = jnp.full((tq, tq), INT_MIN, I32)

    def bit_step(it, state):
        t_u, n_sel = state
        bit = jnp.left_shift(np.int32(1), 31 - it)
        cand_u = t_u | bit
        cand_s = cand_u ^ INT_MIN

        def count_pair(i, cnt):
            start = pl.multiple_of(i * (2 * tq), 2 * tq)
            for j in range(2 * tq // 64):
                k = key_scr[pl.ds(start + j * 64, 64), :]
                cnt = cnt + jnp.where(k >= cand_s, 1.0, 0.0)
            return cnt

        cnt = lax.fori_loop(0, (nch + 1) // 2, count_pair, jnp.zeros((64, tq), F32))
        total = jnp.sum(cnt, axis=0, keepdims=True)
        accept = total >= float(topk)
        return jnp.where(accept, cand_u, t_u), jnp.where(accept, total, n_sel)

    t_u, n_sel = lax.fori_loop(0, 32, bit_step, (jnp.zeros((1, tq), I32), jnp.zeros((1, tq), F32)))
    thr = jnp.maximum(t_u ^ INT_MIN, INT_MIN + np.int32(1))

    @pl.when(jnp.max(n_sel) > float(topk))
    def _():
        def count_greater(c, cnt):
            k = key_scr[pl.ds(pl.multiple_of(c * tq, tq), tq), :]
            return cnt + jnp.sum(jnp.where(k > thr, 1.0, 0.0), axis=0, keepdims=True)

        n_gt = lax.fori_loop(0, nch, count_greater, jnp.zeros((1, tq), F32))
        keep = jnp.where(n_sel > float(topk), float(topk) - n_gt, float(seq))
        tri = jnp.where(key_i >= qry_i, 1.0, 0.0).astype(BF16)

        def drop_late_ties(c, seen):
            start = pl.multiple_of(c * tq, tq)
            k = key_scr[pl.ds(start, tq), :]
            tied = k == thr
            rank = _dot(tri, jnp.where(tied, 1.0, 0.0).astype(BF16)) + seen
            key_scr[pl.ds(start, tq), :] = jnp.where(jnp.logical_and(tied, rank > keep), thr - 1, k)
            return rank[tq - 1:tq, :]

        lax.fori_loop(0, nch, drop_late_ties, jnp.zeros((1, tq), F32))

    m_scr[...] = jnp.full(m_scr.shape, NEG_INIT, F32)
    l_scr[...] = jnp.zeros(l_scr.shape, F32)
    acc_scr[...] = jnp.zeros(acc_scr.shape, F32)

    def attn_chunk(c, carry):
        start = pl.multiple_of(c * tq, tq)
        kk = ka_ref[pl.ds(start, tq), :]
        vt = vt_scr[0, :, pl.ds(start, tq)]
        sel = key_scr[pl.ds(start, tq), :] >= thr
        for g0 in range(0, N_HEADS_A, head_group):
            group = range(g0, min(g0 + head_group, N_HEADS_A))
            scores = {h: _dot_nt(kk, qa_ref[:, h * LANES:(h + 1) * LANES]) for h in group}
            probs, alphas = {}, {}
            for h in group:
                sm = jnp.where(sel, scores[h], NEG_MASKED)
                m_old = m_scr[h, 0:1, :]
                m_new = jnp.maximum(m_old, jnp.max(sm, axis=0, keepdims=True))
                alphas[h] = jnp.exp(m_old - m_new)
                p = jnp.exp(sm - m_new)
                l_scr[h] = jnp.broadcast_to(alphas[h] * l_scr[h, 0:1, :] + jnp.sum(p, axis=0, keepdims=True), (8, tq))
                m_scr[h] = jnp.broadcast_to(m_new, (8, tq))
                probs[h] = p.astype(BF16)
            for h in group:
                acc_scr[h] = alphas[h] * acc_scr[h] + _dot(vt, probs[h])
        return carry

    lax.fori_loop(0, nch, attn_chunk, 0)
    for h in range(N_HEADS_A):
        o_ref[:, h * LANES:(h + 1) * LANES] = (acc_scr[h] / l_scr[h, 0:1, :]).T.astype(BF16)


def _dsa(main3, iwt, tq, head_group):
    b, s, _ = main3.shape
    topk = min(TOPK_MAX, s // 4)
    a_q = N_HEADS_A * HEAD_DIM
    iq_w = IDX_HEADS * IDX_DIM
    assert s % tq == 0 and s % 256 == 0

    def slot(off):
        return lambda bi, qi: (bi, 0, off // LANES)

    return pl.pallas_call(
        functools.partial(_dsa_kernel, tq=tq, topk=topk, seq=s, head_group=head_group),
        grid=(b, s // tq),
        in_specs=[
            pl.BlockSpec((None, tq, iq_w), lambda bi, qi: (bi, qi, L0_IQ // iq_w)),
            pl.BlockSpec((None, IDX_HEADS, tq), lambda bi, qi: (bi, 0, qi)),
            pl.BlockSpec((None, s, LANES), slot(L0_IKLO)),
            pl.BlockSpec((None, s, LANES), slot(L0_IKHI)),
            pl.BlockSpec((None, tq, a_q), lambda bi, qi: (bi, qi, L0_QA // a_q)),
            pl.BlockSpec((None, s, LANES), slot(L0_KA)),
            pl.BlockSpec((None, s, LANES), slot(L0_VA)),
        ],
        out_specs=pl.BlockSpec((None, tq, a_q), lambda bi, qi: (bi, qi, 0)),
        out_shape=jax.ShapeDtypeStruct((b, s, a_q), BF16),
        scratch_shapes=[
            pltpu.VMEM((s + tq, tq), I32),
            pltpu.VMEM((1, LANES, s), BF16),
            pltpu.VMEM((N_HEADS_A, 8, tq), F32),
            pltpu.VMEM((N_HEADS_A, 8, tq), F32),
            pltpu.VMEM((N_HEADS_A, LANES, tq), F32),
        ],
        compiler_params=pltpu.CompilerParams(
            dimension_semantics=("parallel", "arbitrary"), vmem_limit_bytes=VMEM_LIMIT),
        name="dsa",
    )(main3, iwt, main3, main3, main3, main3, main3)


def _dilated_kernel(q_ref, k_ref, v_ref, o_ref, qf, kf, vf, o0, o1, o2, e0, e1, e2, *, seq, blk, unroll):
    qf[...] = q_ref[...].astype(F32)
    kf[...] = k_ref[...].astype(F32)
    vf[...] = v_ref[...].astype(F32)
    o_scr = (o0, o1, o2)
    e_scr = (e0, e1, e2)
    row = lax.broadcasted_iota(I32, (blk, blk), 0)
    col = lax.broadcasted_iota(I32, (blk, blk), 1)

    for pat, (window, dil) in enumerate(DILATED):
        steps = window // dil
        length = seq // dil
        nb = length // blk
        cur_ok = jnp.logical_and(row - col >= 0, row - col <= steps)
        prev_band = jnp.logical_and(row + blk - col >= 0, row + blk - col <= steps)

        def rows(ref, n, r, dil=dil):
            start = n * (blk * dil) + r
            if dil == 1:
                return ref[pl.ds(start, blk), :]
            return ref[pl.ds(start, blk, stride=dil), :]

        def step(it, carry, pat=pat, dil=dil, nb=nb, cur_ok=cur_ok, prev_band=prev_band, rows=rows):
            ids = [it * unroll + u for u in range(unroll)]
            rn = [(i // nb, i % nb) for i in ids]
            ops = []
            for r, n in rn:
                n_prev = jnp.maximum(n - 1, 0)
                ops.append((rows(qf, n, r).astype(BF16), rows(kf, n, r).astype(BF16),
                            rows(kf, n_prev, r).astype(BF16), rows(vf, n, r).astype(BF16),
                            rows(vf, n_prev, r).astype(BF16)))
            scores = [(_dot_nt(q, k_cur), _dot_nt(q, k_prev)) for q, k_cur, k_prev, _, _ in ops]
            probs, stats = [], []
            for (r, n), (sc, sp) in zip(rn, scores):
                s_cur = jnp.where(cur_ok, sc, NEG_MASKED)
                s_prev = jnp.where(jnp.logical_and(prev_band, n > 0), sp, NEG_MASKED)
                m = jnp.max(jnp.maximum(s_cur, s_prev), axis=1, keepdims=True)
                p_cur = jnp.exp(s_cur - m)
                p_prev = jnp.exp(s_prev - m)
                l = jnp.sum(p_cur + p_prev, axis=1, keepdims=True)
                probs.append((p_cur.astype(BF16), p_prev.astype(BF16)))
                stats.append((m, l))
            for (r, n), (p_cur, p_prev), (m, l), (_, _, _, v_cur, v_prev) in zip(rn, probs, stats, ops):
                o = (_dot(p_cur, v_cur) + _dot(p_prev, v_prev)) / l
                lse = jnp.broadcast_to(m + jnp.log(l), (blk, LANES))
                start = n * (blk * dil) + r
                if dil == 1:
                    o_scr[pat][pl.ds(start, blk), :] = o
                    e_scr[pat][pl.ds(start, blk), :] = lse
                else:
                    o_scr[pat][pl.ds(start, blk, stride=dil), :] = o
                    e_scr[pat][pl.ds(start, blk, stride=dil), :] = lse
            return carry

        assert (dil * nb) % unroll == 0
        lax.fori_loop(0, dil * nb // unroll, step, 0)

    rows_per = 512
    def merge(i, carry):
        sl = pl.ds(pl.multiple_of(i * rows_per, rows_per), rows_per)
        es = [e[sl, :] for e in e_scr]
        mx = jnp.maximum(jnp.maximum(es[0], es[1]), es[2])
        ws = [jnp.exp(e - mx) for e in es]
        num = ws[0] * o_scr[0][sl, :] + ws[1] * o_scr[1][sl, :] + ws[2] * o_scr[2][sl, :]
        o_ref[sl, :] = (num / (ws[0] + ws[1] + ws[2])).astype(BF16)
        return carry

    lax.fori_loop(0, seq // rows_per, merge, 0)


def _dilated(main3):
    b, s, _ = main3.shape
    blk = 128
    max_dil = max(d for _, d in DILATED)
    assert s % (blk * max_dil) == 0 and s % 512 == 0

    def slot(off):
        return lambda bi, h: (bi, 0, off // LANES + h)

    return pl.pallas_call(
        functools.partial(_dilated_kernel, seq=s, blk=blk, unroll=4),
        grid=(b, N_HEADS_B),
        in_specs=[
            pl.BlockSpec((None, s, LANES), slot(L0_QB)),
            pl.BlockSpec((None, s, LANES), slot(L0_KB)),
            pl.BlockSpec((None, s, LANES), slot(L0_VB)),
        ],
        out_specs=pl.BlockSpec((None, s, LANES), lambda bi, h: (bi, 0, h)),
        out_shape=jax.ShapeDtypeStruct((b, s, N_HEADS_B * HEAD_DIM), BF16),
        scratch_shapes=[pltpu.VMEM((s, LANES), F32) for _ in range(9)],
        compiler_params=pltpu.CompilerParams(
            dimension_semantics=("parallel", "parallel"), vmem_limit_bytes=VMEM_LIMIT),
        name="dilated",
    )(main3, main3, main3)


def _stick_kernel(q_ref, k_ref, v_ref, o_ref, vt_scr, run_scr, acc_scr, z_scr, *, tq, ck, hg, seq):
    qi = pl.program_id(2)
    nch = ((qi + 1) * tq) // ck
    n_diag = tq // ck

    @pl.when(qi == 0)
    def _():
        _transpose_values(v_ref, vt_scr, hg, seq)

    rr = lax.broadcasted_iota(I32, (ck + 8, 2 * ck), 0)
    cc = lax.broadcasted_iota(I32, (ck + 8, 2 * ck), 1) % ck
    later = jnp.where(jnp.logical_or(cc >= rr, rr >= ck), 1.0, 0.0).astype(BF16)
    dsl = (lax.broadcasted_iota(I32, (ck, tq), 0) - lax.broadcasted_iota(I32, (ck, tq), 1))
    heads = range(hg)
    hs = [slice(h * LANES, (h + 1) * LANES) for h in heads]

    run_scr[...] = jnp.zeros(run_scr.shape, F32)
    acc_scr[...] = jnp.zeros(acc_scr.shape, F32)

    def logits(c):
        start = pl.multiple_of(jnp.maximum(c, 0) * ck, ck)
        return [_dot_nt(k_ref[pl.ds(start, ck), hs[h]], q_ref[:, hs[h]]) for h in heads]

    def chunk(c, zs, masked):
        start = pl.multiple_of(c * ck, ck)
        before = dsl < (qi * tq - c * ck)
        stacked = []
        for h in heads:
            z = zs[h]
            sp = jnp.maximum(z, 0.0) + jnp.log(1.0 + jnp.exp(-jnp.abs(z)))
            if masked:
                sp = jnp.where(before, sp, 0.0)
            hi = pltpu.bitcast(pltpu.bitcast(sp, I32) & np.int32(-65536), F32)
            stacked.append(jnp.concatenate([hi.astype(BF16), (sp - hi).astype(BF16)], axis=0))
        sums = [_dot(later, stacked[h]) for h in heads]
        ws = []
        for h in heads:
            run = run_scr[h]
            w = jnp.exp(zs[h] - (sums[h][:ck] + run[0:1, :]))
            if masked:
                w = jnp.where(before, w, 0.0)
            run_scr[h] = run + sums[h][ck:]
            ws.append(w.astype(BF16))
        for h in heads:
            acc_scr[h] += _dot(vt_scr[h, :, pl.ds(start, ck)], ws[h])

    first = nch - 1
    zs = logits(first)
    for d in range(n_diag):
        nxt = logits(first - d - 1)
        chunk(first - d, zs, True)
        zs = nxt
    for h in heads:
        z_scr[0, h] = zs[h]

    def off_diagonal(it, carry):
        c = first - n_diag - it
        slot = it % 2
        zs = [z_scr[slot, h] for h in heads]
        nxt = logits(c - 1)
        chunk(c, zs, False)
        for h in heads:
            z_scr[1 - slot, h] = nxt[h]
        return carry

    lax.fori_loop(0, nch - n_diag, off_diagonal, 0)
    for h in heads:
        o_ref[:, hs[h]] = acc_scr[h].T.astype(BF16)


def _stick(main3, tq, ck, hg):
    b, s, _ = main3.shape
    gw = hg * HEAD_DIM
    assert N_HEADS_C % hg == 0 and tq % ck == 0 and s % tq == 0 and s % 256 == 0
    return pl.pallas_call(
        functools.partial(_stick_kernel, tq=tq, ck=ck, hg=hg, seq=s),
        grid=(b, N_HEADS_C // hg, s // tq),
        in_specs=[
            pl.BlockSpec((None, tq, gw), lambda bi, g, qi: (bi, qi, L1_QC // gw + g)),
            pl.BlockSpec((None, s, gw), lambda bi, g, qi: (bi, 0, L1_KC // gw + g)),
            pl.BlockSpec((None, s, gw), lambda bi, g, qi: (bi, 0, L1_VC // gw + g)),
        ],
        out_specs=pl.BlockSpec((None, tq, gw), lambda bi, g, qi: (bi, qi, g)),
        out_shape=jax.ShapeDtypeStruct((b, s, N_HEADS_C * HEAD_DIM), BF16),
        scratch_shapes=[
            pltpu.VMEM((hg, LANES, s), BF16),
            pltpu.VMEM((hg, 8, tq), F32),
            pltpu.VMEM((hg, LANES, tq), F32),
            pltpu.VMEM((2, hg, ck, tq), F32),
        ],
        compiler_params=pltpu.CompilerParams(
            dimension_semantics=("parallel", "parallel", "arbitrary"), vmem_limit_bytes=VMEM_LIMIT),
        name="stick",
    )(main3, main3, main3)


def _split3(x):
    x1 = x.astype(BF16)
    r1 = x - x1.astype(F32)
    x2 = r1.astype(BF16)
    x3 = (r1 - x2.astype(F32)).astype(BF16)
    return x1, x2, x3


def _forget_cumsum_kernel(fl_ref, b_ref, c_ref, *, heads, rows):
    x = fl_ref[...] + b_ref[...]
    lf = _log_sigmoid(x).reshape(heads * rows, LANES)
    jj = lax.broadcasted_iota(I32, (LANES, 2 * LANES), 0)
    ss = lax.broadcasted_iota(I32, (LANES, 2 * LANES), 1)
    upper = jnp.where(jnp.logical_or(jj <= ss, ss >= LANES), 1.0, 0.0).astype(BF16)
    both = sum(_dot(part, upper) for part in _split3(lf))
    within = both[:, :LANES]
    total = both[:, LANES:]
    n = heads * rows
    aa = lax.broadcasted_iota(I32, (n, n), 0)
    bb = lax.broadcasted_iota(I32, (n, n), 1)
    same_head = (aa // rows) == (bb // rows)
    lower = jnp.where(jnp.logical_and(same_head, bb < aa), 1.0, 0.0).astype(BF16)
    offs = sum(_dot(lower, part) for part in _split3(total))
    c_ref[...] = (within + offs).reshape(heads, rows, LANES)


def _forget_cumsum(fl4, bf3):
    b, heads, rows, _ = fl4.shape
    return pl.pallas_call(
        functools.partial(_forget_cumsum_kernel, heads=heads, rows=rows),
        grid=(b,),
        in_specs=[
            pl.BlockSpec((None, heads, rows, LANES), lambda bi: (bi, 0, 0, 0)),
            pl.BlockSpec((heads, 1, LANES), lambda bi: (0, 0, 0)),
        ],
        out_specs=pl.BlockSpec((None, heads, rows, LANES), lambda bi: (bi, 0, 0, 0)),
        out_shape=jax.ShapeDtypeStruct(fl4.shape, F32),
        compiler_params=pltpu.CompilerParams(dimension_semantics=("parallel",)),
        name="forget_cumsum",
    )(fl4, bf3)


def _fox_kernel(q_ref, k_ref, v_ref, crow_ref, o_ref, vt_scr, cb_scr, m_scr, l_scr, acc_scr, s_scr,
                *, tq, hg, seq):
    qi = pl.program_id(2)
    heads = range(hg)
    hs = [slice(h * LANES, (h + 1) * LANES) for h in heads]
    reps = tq // LANES

    @pl.when(qi == 0)
    def _():
        _transpose_values(v_ref, vt_scr, hg, seq)

        def body(i, carry):
            start = pl.multiple_of(i * LANES, LANES)
            for h in heads:
                row = crow_ref[h, :, pl.ds(start, LANES)]
                cb_scr[h, pl.ds(start, LANES), :] = jnp.broadcast_to(row, (LANES, LANES)).T
            return carry

        lax.fori_loop(0, seq // LANES, body, 0)

    m_scr[...] = jnp.full(m_scr.shape, NEG_INIT, F32)
    l_scr[...] = jnp.zeros(l_scr.shape, F32)
    acc_scr[...] = jnp.zeros(acc_scr.shape, F32)
    c_t = [crow_ref[h, :, pl.ds(pl.multiple_of(qi * tq, tq), tq)] for h in heads]

    def logits(c):
        start = pl.multiple_of(c * tq, tq)
        return [_dot_nt(k_ref[pl.ds(start, tq), hs[h]], q_ref[:, hs[h]]) for h in heads]

    def chunk(c, scores, diagonal):
        start = pl.multiple_of(c * tq, tq)
        probs, alphas = [], []
        for h in heads:
            cb = cb_scr[h, pl.ds(start, tq), :]
            u = scores[h] - (jnp.concatenate([cb] * reps, axis=1) if reps > 1 else cb)
            if diagonal:
                key = lax.broadcasted_iota(I32, (tq, tq), 0)
                qry = lax.broadcasted_iota(I32, (tq, tq), 1)
                u = jnp.where(key <= qry, u, NEG_MASKED)
            m_old = m_scr[h, 0:1, :]
            m_new = jnp.maximum(m_old, jnp.max(u, axis=0, keepdims=True) + c_t[h])
            alpha = jnp.exp(m_old - m_new)
            p = jnp.exp(u - (m_new - c_t[h]))
            l_scr[h] = jnp.broadcast_to(alpha * l_scr[h, 0:1, :] + jnp.sum(p, axis=0, keepdims=True), (8, tq))
            m_scr[h] = jnp.broadcast_to(m_new, (8, tq))
            probs.append(p.astype(BF16))
            alphas.append(alpha)
        for h in heads:
            acc_scr[h] = alphas[h] * acc_scr[h] + _dot(vt_scr[h, :, pl.ds(start, tq)], probs[h])

    first = logits(0)
    for h in heads:
        s_scr[0, h] = first[h]

    def off_diagonal(c, carry):
        slot = c % 2
        scores = [s_scr[slot, h] for h in heads]
        nxt = logits(c + 1)
        chunk(c, scores, False)
        for h in heads:
            s_scr[1 - slot, h] = nxt[h]
        return carry

    lax.fori_loop(0, qi, off_diagonal, 0)
    chunk(qi, [s_scr[qi % 2, h] for h in heads], True)
    for h in heads:
        o_ref[:, hs[h]] = (acc_scr[h] / l_scr[h, 0:1, :]).T.astype(BF16)


def _fox(main3, crow, tq, hg):
    b, s, _ = main3.shape
    gw = hg * HEAD_DIM
    assert N_HEADS_D % hg == 0 and s % tq == 0 and s % 256 == 0
    return pl.pallas_call(
        functools.partial(_fox_kernel, tq=tq, hg=hg, seq=s),
        grid=(b, N_HEADS_D // hg, s // tq),
        in_specs=[
            pl.BlockSpec((None, tq, gw), lambda bi, g, qi: (bi, qi, L1_QD // gw + g)),
            pl.BlockSpec((None, s, gw), lambda bi, g, qi: (bi, 0, L1_KD // gw + g)),
            pl.BlockSpec((None, s, gw), lambda bi, g, qi: (bi, 0, L1_VD // gw + g)),
            pl.BlockSpec((None, hg, 1, s), lambda bi, g, qi: (bi, g, 0, 0)),
        ],
        out_specs=pl.BlockSpec((None, tq, gw), lambda bi, g, qi: (bi, qi, g)),
        out_shape=jax.ShapeDtypeStruct((b, s, N_HEADS_D * HEAD_DIM), BF16),
        scratch_shapes=[
            pltpu.VMEM((hg, LANES, s), BF16),
            pltpu.VMEM((hg, s, LANES), F32),
            pltpu.VMEM((hg, 8, tq), F32),
            pltpu.VMEM((hg, 8, tq), F32),
            pltpu.VMEM((hg, LANES, tq), F32),
            pltpu.VMEM((2, hg, tq, tq), F32),
        ],
        compiler_params=pltpu.CompilerParams(
            dimension_semantics=("parallel", "parallel", "arbitrary"), vmem_limit_bytes=VMEM_LIMIT),
        name="fox",
    )(main3, main3, main3, crow)


def _outproj_kernel(x_ref, ya_ref, yb_ref, gate_ref, w_ref, gf_ref, o_ref, *, na, final_norm):
    g = gate_ref[...].astype(F32)
    silu = g / (1.0 + jnp.exp(-g))
    ya = (ya_ref[...].astype(F32) * silu[:, :na]).astype(BF16)
    yb = (yb_ref[...].astype(F32) * silu[:, na:]).astype(BF16)
    out = x_ref[...] + _dot(ya, w_ref[:na, :]) + _dot(yb, w_ref[na:, :])
    if final_norm:
        ms = jnp.mean(out * out, axis=-1, keepdims=True)
        out = (out * lax.rsqrt(ms + EPS)) * gf_ref[...]
    o_ref[...] = out


def _outproj(x2, ya, yb, main2, w, gf, final_norm, tm):
    m, d = x2.shape
    na, nb = ya.shape[1], yb.shape[1]
    assert na + nb == w.shape[0] and m % tm == 0
    return pl.pallas_call(
        functools.partial(_outproj_kernel, na=na, final_norm=final_norm),
        grid=(m // tm,),
        in_specs=[
            pl.BlockSpec((tm, d), lambda i: (i, 0)),
            pl.BlockSpec((tm, na), lambda i: (i, 0)),
            pl.BlockSpec((tm, nb), lambda i: (i, 0)),
            pl.BlockSpec((tm, na + nb), lambda i: (i, 0)),
            pl.BlockSpec(w.shape, lambda i: (0, 0)),
            pl.BlockSpec((1, d), lambda i: (0, 0)),
        ],
        out_specs=pl.BlockSpec((tm, d), lambda i: (i, 0)),
        out_shape=jax.ShapeDtypeStruct((m, d), F32),
        compiler_params=pltpu.CompilerParams(
            dimension_semantics=("parallel",), vmem_limit_bytes=VMEM_LIMIT),
        name="outproj",
    )(x2, ya, yb, main2, w, gf.reshape(1, d))


def _split_cols(w, sizes):
    offs = np.cumsum(sizes)[:-1].tolist()
    return jnp.split(w, offs, axis=1)


def _layer0_weights(w_in0):
    d = w_in0.shape[0]
    a_q, iq_w, b_w = N_HEADS_A * HEAD_DIM, IDX_HEADS * IDX_DIM, N_HEADS_B * HEAD_DIM
    qa, ka, va, iq, ik, iw, qb, kb, vb, gate = _split_cols(
        w_in0, (a_q, HEAD_DIM, HEAD_DIM, iq_w, IDX_DIM, IDX_HEADS, b_w, b_w, b_w, d))
    z = lambda n: jnp.zeros((d, n), w_in0.dtype)
    iklo = jnp.concatenate([ik, iw, z(LANES - IDX_DIM - IDX_HEADS)], axis=1)
    ikhi = jnp.concatenate([z(LANES - IDX_DIM), ik], axis=1)
    w = jnp.concatenate([gate, iq, qb, qa, ka, va, iklo, ikhi, kb, vb], axis=1)
    assert w.shape[1] == L0_WIDTH
    blocks = ([(E_NONE, E_NONE)] * 8 + [(E_IDX64, E_IDX64)] * 4 + [(E_Q128, E_Q128)] * 3
              + [(E_Q128, E_Q128)] * 5 + [(E_K128, E_NONE)] + [(E_IKLO, E_IDX64)]
              + [(E_K128, E_K128)] * 3 + [(E_NONE, E_NONE)] * 3)
    return w.astype(BF16), blocks, L0_IKLO // MXU_COLS


def _layer1_weights(w_in1):
    d = w_in1.shape[0]
    c_w, d_w = N_HEADS_C * HEAD_DIM, N_HEADS_D * HEAD_DIM
    qc, kc, vc, qd, kd, vd, fl, gate = _split_cols(w_in1, (c_w, c_w, c_w, d_w, d_w, d_w, N_HEADS_D, d))
    flp = jnp.concatenate([fl, jnp.zeros((d, MXU_COLS - N_HEADS_D), w_in1.dtype)], axis=1)
    w = jnp.concatenate([gate, qc, kc, vc, qd, kd, vd, flp], axis=1)
    assert w.shape[1] == L1_WIDTH
    blocks = ([(E_NONE, E_NONE)] * 8 + [(E_QSCALE, E_QSCALE)] * 4 + [(E_NONE, E_NONE)] * 8
              + [(E_QSCALE, E_QSCALE)] * 4 + [(E_NONE, E_NONE)] * 8 + [(E_NONE, E_NONE)])
    return w.astype(BF16), blocks, L1_FL // MXU_COLS


def kernel(x, norm0, w_in0, w_out0, norm1, w_in1, b_f1, w_out1, norm_f):
    b, s, d = x.shape
    m = b * s
    x2 = x.reshape(m, d)
    tm_in = min(256, s)
    tm_out = min(256, s)

    w0, blocks0, small0 = _layer0_weights(w_in0)
    main0, small0_out = _inproj(x2, norm0, w0, _rope_tables(s), blocks0, small0, s, tm_in)
    main0_3 = main0.reshape(b, s, L0_WIDTH)
    iwt = small0_out[:, IDX_DIM:IDX_DIM + IDX_HEADS].reshape(b, s, IDX_HEADS).transpose(0, 2, 1)
    y_a = _dsa(main0_3, iwt, tq=256, head_group=5)
    y_b = _dilated(main0_3)
    x2 = _outproj(x2, y_a.reshape(m, -1), y_b.reshape(m, -1), main0, w_out0.astype(BF16), norm_f,
                  final_norm=False, tm=tm_out)

    w1, blocks1, small1 = _layer1_weights(w_in1)
    main1, small1_out = _inproj(x2, norm1, w1, None, blocks1, small1, s, tm_in)
    main1_3 = main1.reshape(b, s, L1_WIDTH)
    y_c = _stick(main1_3, tq=256, ck=128, hg=8)
    fl = small1_out[:, :N_HEADS_D].reshape(b, s, N_HEADS_D).transpose(0, 2, 1)
    c = _forget_cumsum(fl.reshape(b, N_HEADS_D, s // LANES, LANES),
                       jnp.broadcast_to(b_f1.astype(F32)[:, None, None], (N_HEADS_D, 1, LANES)))
    crow = c.reshape(b, N_HEADS_D, 1, s)
    y_d = _fox(main1_3, crow, tq=256, hg=4)
    out = _outproj(x2, y_c.reshape(m, -1), y_d.reshape(m, -1), main1, w_out1.astype(BF16), norm_f,
                   final_norm=True, tm=tm_out)
    return out.reshape(b, s, d)
```

```python
import functools
import math

import numpy as np
import jax
import jax.numpy as jnp
from jax import lax
from jax.experimental import pallas as pl
from jax.experimental.pallas import tpu as pltpu

F32 = jnp.float32
BF16 = jnp.bfloat16
I32 = jnp.int32

HEAD_DIM = 128
N_HEADS_A = 10
N_HEADS_B = 6
N_HEADS_C = 8
N_HEADS_D = 8
IDX_HEADS = 16
IDX_DIM = 64
TOPK_MAX = 256
DILATED = ((128, 1), (512, 4), (2048, 16))
ROPE_THETA = 10000.0
EPS = 1e-6

LANES = 128
MXU_COLS = 256
BF16_ROWS = 16
VMEM_LIMIT = 56 * 1024 * 1024

NEG_MASKED = -1.0e30
NEG_INIT = -0.5e30
INT_MIN = np.int32(-2 ** 31)

L0_GATE, L0_IQ, L0_QB, L0_QA = 0, 2048, 3072, 3840
L0_KA, L0_VA, L0_IKLO, L0_IKHI, L0_KB, L0_VB = 5120, 5248, 5376, 5504, 5632, 6400
L0_WIDTH = 7168
L1_GATE, L1_QC, L1_KC, L1_VC, L1_QD, L1_KD, L1_VD, L1_FL = 0, 2048, 3072, 4096, 5120, 6144, 7168, 8192
L1_WIDTH = 8448

E_NONE, E_Q128, E_K128, E_QSCALE, E_IDX64, E_IKLO = range(6)


def _dot_nt(a, b):
    return lax.dot_general(a, b, (((1,), (1,)), ((), ())), preferred_element_type=F32)


def _dot(a, b):
    return jnp.dot(a, b, preferred_element_type=F32)


def _log_sigmoid(z):
    return jnp.minimum(z, 0.0) - jnp.log(1.0 + jnp.exp(-jnp.abs(z)))


def _rope128(t, cos, sin):
    return t * cos + pltpu.roll(t, 64, 1) * sin


def _rope64(t, cos, sin_up, sin_dn):
    return t * cos + pltpu.roll(t, 96, 1) * sin_up + pltpu.roll(t, 32, 1) * sin_dn


def _inproj_kernel(*refs, blocks, small_block, use_rope):
    if use_rope:
        x_ref, g_ref, w_ref, c128, s128, c64, su64, sd64, o_ref, small_ref, h_scr = refs
    else:
        x_ref, g_ref, w_ref, o_ref, small_ref, h_scr = refs
    x = x_ref[...]
    ms = jnp.mean(x * x, axis=-1, keepdims=True)
    h_scr[...] = ((x * lax.rsqrt(ms + EPS)) * g_ref[...]).astype(BF16)
    scale = HEAD_DIM ** -0.5

    def half(a, kind):
        if kind == E_NONE:
            f = a
        elif kind == E_Q128:
            f = _rope128(a * scale, c128[...], s128[...])
        elif kind == E_K128:
            f = _rope128(a, c128[...], s128[...])
        elif kind == E_QSCALE:
            f = a * scale
        elif kind == E_IDX64:
            f = _rope64(a, c64[...], su64[...], sd64[...])
        elif kind == E_IKLO:
            lane = lax.broadcasted_iota(I32, a.shape, 1)
            r = _rope64(a, c64[...], su64[...], sd64[...])
            f = jnp.where(lane < IDX_DIM, r, a)
            return f, jnp.where(lane < IDX_DIM, r, 0.0).astype(BF16)
        else:
            raise ValueError(kind)
        return f, f.astype(BF16)

    for jj, kinds in enumerate(blocks):
        c0 = jj * MXU_COLS
        acc = _dot(h_scr[...], w_ref[:, c0:c0 + MXU_COLS])
        f0, b0 = half(acc[:, :LANES], kinds[0])
        f1, b1 = half(acc[:, LANES:], kinds[1])
        o_ref[:, c0:c0 + LANES] = b0
        o_ref[:, c0 + LANES:c0 + MXU_COLS] = b1
        if jj == small_block:
            small_ref[:, :LANES] = f0
            small_ref[:, LANES:] = f1


def _inproj(x2, g, w, tables, blocks, small_block, seq, tm):
    m, d = x2.shape
    n = w.shape[1]
    assert n == len(blocks) * MXU_COLS and m % tm == 0 and seq % tm == 0
    use_rope = tables is not None
    per_seq = seq // tm
    in_specs = [
        pl.BlockSpec((tm, d), lambda i: (i, 0)),
        pl.BlockSpec((1, d), lambda i: (0, 0)),
        pl.BlockSpec((d, n), lambda i: (0, 0), pipeline_mode=pl.Buffered(1)),
    ]
    args = [x2, g.reshape(1, d), w]
    if use_rope:
        for t in tables:
            in_specs.append(pl.BlockSpec((tm, LANES), lambda i: (i % per_seq, 0)))
            args.append(t)
    return pl.pallas_call(
        functools.partial(_inproj_kernel, blocks=tuple(blocks), small_block=small_block, use_rope=use_rope),
        grid=(m // tm,),
        in_specs=in_specs,
        out_specs=[
            pl.BlockSpec((tm, n), lambda i: (i, 0)),
            pl.BlockSpec((tm, MXU_COLS), lambda i: (i, 0)),
        ],
        out_shape=[
            jax.ShapeDtypeStruct((m, n), BF16),
            jax.ShapeDtypeStruct((m, MXU_COLS), F32),
        ],
        scratch_shapes=[pltpu.VMEM((tm, d), BF16)],
        compiler_params=pltpu.CompilerParams(
            dimension_semantics=("parallel",), vmem_limit_bytes=VMEM_LIMIT),
        name="inproj",
    )(*args)


def _rope_tables(seq):
    pos = jnp.arange(seq, dtype=F32)[:, None]
    half = HEAD_DIM // 2
    inv = ROPE_THETA ** (-jnp.arange(half, dtype=F32) / half)
    ang = pos * inv[None, :]
    c128 = jnp.concatenate([jnp.cos(ang), jnp.cos(ang)], axis=1)
    s128 = jnp.concatenate([-jnp.sin(ang), jnp.sin(ang)], axis=1)
    half = IDX_DIM // 2
    inv = ROPE_THETA ** (-jnp.arange(half, dtype=F32) / half)
    ang = pos * inv[None, :]
    zero = jnp.zeros_like(ang)
    c64 = jnp.concatenate([jnp.cos(ang)] * 4, axis=1)
    su64 = jnp.concatenate([-jnp.sin(ang), zero] * 2, axis=1)
    sd64 = jnp.concatenate([zero, jnp.sin(ang)] * 2, axis=1)
    return c128, s128, c64, su64, sd64


def _transpose_values(v_ref, vt_scr, hg, seq):
    piece = 256
    eye = (lax.broadcasted_iota(I32, (LANES, LANES), 0) == lax.broadcasted_iota(I32, (LANES, LANES), 1))
    eye = jnp.where(eye, 1.0, 0.0).astype(BF16)

    extra = vt_scr.shape[1] - LANES

    def body(i, carry):
        start = pl.multiple_of(i * piece, piece)
        for h in range(hg):
            vt = _dot_nt(eye, v_ref[pl.ds(start, piece), h * LANES:(h + 1) * LANES])
            vt_scr[h, 0:LANES, pl.ds(start, piece)] = vt.astype(BF16)
            if extra:
                vt_scr[h, LANES:LANES + extra, pl.ds(start, piece)] = jnp.ones((extra, piece), BF16)
        return carry

    lax.fori_loop(0, seq // piece, body, 0)


def _dsa_kernel(iq_ref, iwt_ref, iklo_ref, ikhi_ref, qa_ref, ka_ref, va_ref, o_ref,
                key_scr, vt_scr, m_scr, l_scr, acc_scr, s_scr, p_scr, a_scr, *, tq, topk, seq):
    qi = pl.program_id(1)
    nch = qi + 1
    reps = tq // LANES
    key_i = lax.broadcasted_iota(I32, (tq, tq), 0)
    qry_i = lax.broadcasted_iota(I32, (tq, tq), 1)

    @pl.when(qi == 0)
    def _():
        _transpose_values(va_ref, vt_scr, 1, seq)

    wscale = (IDX_DIM ** -0.5) * (IDX_HEADS ** -0.5)
    w = iwt_ref[...] * wscale

    def score_chunk(c, carry):
        start = pl.multiple_of(c * tq, tq)
        klo = iklo_ref[pl.ds(start, tq), :]
        khi = ikhi_ref[pl.ds(start, tq), :]
        score = jnp.zeros((tq, tq), F32)
        for p in range(IDX_HEADS // 2):
            iq2 = iq_ref[:, p * LANES:(p + 1) * LANES]
            d0 = _dot_nt(klo, iq2)
            d1 = _dot_nt(khi, iq2)
            score = score + w[2 * p:2 * p + 1, :] * jnp.maximum(d0, 0.0) + w[2 * p + 1:2 * p + 2, :] * jnp.maximum(d1, 0.0)
        bits = pltpu.bitcast(score, I32)
        okey = jnp.where(bits < 0, bits ^ np.int32(0x7FFFFFFF), bits)
        okey = jnp.where(key_i + (c - qi) * tq <= qry_i, okey, INT_MIN)
        key_scr[pl.ds(start, tq), :] = okey
        return carry

    lax.fori_loop(0, nch, score_chunk, 0)
    key_scr[pl.ds(pl.multiple_of(nch * tq, tq), tq), :] = jnp.full((tq, tq), INT_MIN, I32)

    def bit_step(it, state):
        t_u, n_sel = state
        bit = jnp.left_shift(np.int32(1), 31 - it)
        cand_u = t_u | bit
        cand_s = cand_u ^ INT_MIN

        def count_pair(i, cnt):
            start = pl.multiple_of(i * (2 * tq), 2 * tq)
            for j in range(2 * tq // 64):
                k = key_scr[pl.ds(start + j * 64, 64), :]
                cnt = cnt + jnp.where(k >= cand_s, 1.0, 0.0)
            return cnt

        cnt = lax.fori_loop(0, (nch + 1) // 2, count_pair, jnp.zeros((64, tq), F32))
        total = jnp.sum(cnt, axis=0, keepdims=True)
        accept = total >= float(topk)
        return jnp.where(accept, cand_u, t_u), jnp.where(accept, total, n_sel)

    t_u, n_sel = lax.fori_loop(0, 32, bit_step, (jnp.zeros((1, tq), I32), jnp.zeros((1, tq), F32)))
    thr = jnp.maximum(t_u ^ INT_MIN, INT_MIN + np.int32(1))

    @pl.when(jnp.max(n_sel) > float(topk))
    def _():
        def count_greater(c, cnt):
            k = key_scr[pl.ds(pl.multiple_of(c * tq, tq), tq), :]
            return cnt + jnp.sum(jnp.where(k > thr, 1.0, 0.0), axis=0, keepdims=True)

        n_gt = lax.fori_loop(0, nch, count_greater, jnp.zeros((1, tq), F32))
        keep = jnp.where(n_sel > float(topk), float(topk) - n_gt, float(seq))
        tri = jnp.where(key_i >= qry_i, 1.0, 0.0).astype(BF16)

        def drop_late_ties(c, seen):
            start = pl.multiple_of(c * tq, tq)
            k = key_scr[pl.ds(start, tq), :]
            tied = k == thr
            rank = _dot(tri, jnp.where(tied, 1.0, 0.0).astype(BF16)) + seen
            key_scr[pl.ds(start, tq), :] = jnp.where(jnp.logical_and(tied, rank > keep), thr - 1, k)
            return rank[tq - 1:tq, :]

        lax.fori_loop(0, nch, drop_late_ties, jnp.zeros((1, tq), F32))

    m_scr[...] = jnp.full(m_scr.shape, NEG_INIT, F32)
    l_scr[...] = jnp.zeros(l_scr.shape, F32)
    acc_scr[...] = jnp.zeros(acc_scr.shape, F32)

    heads = range(N_HEADS_A)

    def keys_of(c):
        return ka_ref[pl.ds(pl.multiple_of(jnp.minimum(c, nch - 1) * tq, tq), tq), :]

    def logits(kk, h):
        return _dot_nt(kk, qa_ref[:, h * LANES:(h + 1) * LANES])

    def values_of(c):
        return vt_scr[0, :, pl.ds(pl.multiple_of(jnp.maximum(c, 0) * tq, tq), tq)]

    def accumulate(vt, h):
        alpha = a_scr[h, 0:1, :]
        pv = _dot(vt, p_scr[h])
        acc_scr[h] = alpha * acc_scr[h] + pv[:LANES]
        l_scr[h] = jnp.broadcast_to(alpha * l_scr[h, 0:1, :] + pv[LANES:LANES + 1], (8, tq))

    for h in heads:
        s_scr[h] = logits(keys_of(0), h)
        p_scr[h] = jnp.zeros((tq, tq), BF16)
        a_scr[h] = jnp.ones((8, tq), F32)

    def attn_chunk(c, carry):
        sel = key_scr[pl.ds(pl.multiple_of(c * tq, tq), tq), :] >= thr
        vt = values_of(c - 1)
        kk = keys_of(c + 1)
        for h in heads:
            accumulate(vt, h)
            nxt = logits(kk, h)
            sm = jnp.where(sel, s_scr[h], NEG_MASKED)
            m_old = m_scr[h, 0:1, :]
            m_new = jnp.maximum(m_old, jnp.max(sm, axis=0, keepdims=True))
            a_scr[h] = jnp.broadcast_to(jnp.exp(m_old - m_new), (8, tq))
            p_scr[h] = jnp.exp((sm - m_new).astype(BF16))
            m_scr[h] = jnp.broadcast_to(m_new, (8, tq))
            s_scr[h] = nxt
        return carry

    lax.fori_loop(0, nch, attn_chunk, 0)
    vt = values_of(nch - 1)
    for h in heads:
        accumulate(vt, h)
    for h in range(N_HEADS_A):
        o_ref[:, h * LANES:(h + 1) * LANES] = (acc_scr[h] / l_scr[h, 0:1, :]).T.astype(BF16)


def _dsa(main3, iwt, tq):
    b, s, _ = main3.shape
    topk = min(TOPK_MAX, s // 4)
    a_q = N_HEADS_A * HEAD_DIM
    iq_w = IDX_HEADS * IDX_DIM
    assert s % tq == 0 and s % 256 == 0

    def slot(off):
        return lambda bi, qi: (bi, 0, off // LANES)

    return pl.pallas_call(
        functools.partial(_dsa_kernel, tq=tq, topk=topk, seq=s),
        grid=(b, s // tq),
        in_specs=[
            pl.BlockSpec((None, tq, iq_w), lambda bi, qi: (bi, qi, L0_IQ // iq_w)),
            pl.BlockSpec((None, IDX_HEADS, tq), lambda bi, qi: (bi, 0, qi)),
            pl.BlockSpec((None, s, LANES), slot(L0_IKLO)),
            pl.BlockSpec((None, s, LANES), slot(L0_IKHI)),
            pl.BlockSpec((None, tq, a_q), lambda bi, qi: (bi, qi, L0_QA // a_q)),
            pl.BlockSpec((None, s, LANES), slot(L0_KA)),
            pl.BlockSpec((None, s, LANES), slot(L0_VA)),
        ],
        out_specs=pl.BlockSpec((None, tq, a_q), lambda bi, qi: (bi, qi, 0)),
        out_shape=jax.ShapeDtypeStruct((b, s, a_q), BF16),
        scratch_shapes=[
            pltpu.VMEM((s + tq, tq), I32),
            pltpu.VMEM((1, LANES + BF16_ROWS, s), BF16),
            pltpu.VMEM((N_HEADS_A, 8, tq), F32),
            pltpu.VMEM((N_HEADS_A, 8, tq), F32),
            pltpu.VMEM((N_HEADS_A, LANES, tq), F32),
            pltpu.VMEM((N_HEADS_A, tq, tq), F32),
            pltpu.VMEM((N_HEADS_A, tq, tq), BF16),
            pltpu.VMEM((N_HEADS_A, 8, tq), F32),
        ],
        compiler_params=pltpu.CompilerParams(
            dimension_semantics=("parallel", "arbitrary"), vmem_limit_bytes=VMEM_LIMIT),
        name="dsa",
    )(main3, iwt, main3, main3, main3, main3, main3)


def _dilated_kernel(q_ref, k_ref, v_ref, o_ref, qf, kf, vf, o0, o1, o2, e0, e1, e2, *, seq, blk, unroll):
    qf[...] = q_ref[...].astype(F32)
    kf[...] = k_ref[...].astype(F32)
    vf[...] = v_ref[...].astype(F32)
    o_scr = (o0, o1, o2)
    e_scr = (e0, e1, e2)
    row = lax.broadcasted_iota(I32, (blk, blk), 0)
    col = lax.broadcasted_iota(I32, (blk, blk), 1)

    for pat, (window, dil) in enumerate(DILATED):
        steps = window // dil
        length = seq // dil
        nb = length // blk
        cur_ok = jnp.logical_and(row - col >= 0, row - col <= steps)
        prev_band = jnp.logical_and(row + blk - col >= 0, row + blk - col <= steps)

        def rows(ref, n, r, dil=dil):
            start = n * (blk * dil) + r
            if dil == 1:
                return ref[pl.ds(start, blk), :]
            return ref[pl.ds(start, blk, stride=dil), :]

        def step(it, carry, pat=pat, dil=dil, nb=nb, cur_ok=cur_ok, prev_band=prev_band, rows=rows):
            ids = [it * unroll + u for u in range(unroll)]
            rn = [(i // nb, i % nb) for i in ids]
            ops = []
            for r, n in rn:
                n_prev = jnp.maximum(n - 1, 0)
                ops.append((rows(qf, n, r).astype(BF16), rows(kf, n, r).astype(BF16),
                            rows(kf, n_prev, r).astype(BF16), rows(vf, n, r).astype(BF16),
                            rows(vf, n_prev, r).astype(BF16)))
            scores = [(_dot_nt(q, k_cur), _dot_nt(q, k_prev)) for q, k_cur, k_prev, _, _ in ops]
            probs, stats = [], []
            for (r, n), (sc, sp) in zip(rn, scores):
                s_cur = jnp.where(cur_ok, sc, NEG_MASKED)
                s_prev = jnp.where(jnp.logical_and(prev_band, n > 0), sp, NEG_MASKED)
                m = jnp.max(jnp.maximum(s_cur, s_prev), axis=1, keepdims=True)
                p_cur = jnp.exp(s_cur - m)
                p_prev = jnp.exp(s_prev - m)
                l = jnp.sum(p_cur + p_prev, axis=1, keepdims=True)
                probs.append((p_cur.astype(BF16), p_prev.astype(BF16)))
                stats.append((m, l))
            for (r, n), (p_cur, p_prev), (m, l), (_, _, _, v_cur, v_prev) in zip(rn, probs, stats, ops):
                o = (_dot(p_cur, v_cur) + _dot(p_prev, v_prev)) / l
                lse = jnp.broadcast_to(m + jnp.log(l), (blk, LANES))
                start = n * (blk * dil) + r
                if dil == 1:
                    o_scr[pat][pl.ds(start, blk), :] = o
                    e_scr[pat][pl.ds(start, blk), :] = lse
                else:
                    o_scr[pat][pl.ds(start, blk, stride=dil), :] = o
                    e_scr[pat][pl.ds(start, blk, stride=dil), :] = lse
            return carry

        assert (dil * nb) % unroll == 0
        lax.fori_loop(0, dil * nb // unroll, step, 0)

    rows_per = 512
    def merge(i, carry):
        sl = pl.ds(pl.multiple_of(i * rows_per, rows_per), rows_per)
        es = [e[sl, :] for e in e_scr]
        mx = jnp.maximum(jnp.maximum(es[0], es[1]), es[2])
        ws = [jnp.exp(e - mx) for e in es]
        num = ws[0] * o_scr[0][sl, :] + ws[1] * o_scr[1][sl, :] + ws[2] * o_scr[2][sl, :]
        o_ref[sl, :] = (num / (ws[0] + ws[1] + ws[2])).astype(BF16)
        return carry

    lax.fori_loop(0, seq // rows_per, merge, 0)


def _dilated(main3):
    b, s, _ = main3.shape
    blk = 128
    max_dil = max(d for _, d in DILATED)
    assert s % (blk * max_dil) == 0 and s % 512 == 0

    def slot(off):
        return lambda bi, h: (bi, 0, off // LANES + h)

    return pl.pallas_call(
        functools.partial(_dilated_kernel, seq=s, blk=blk, unroll=4),
        grid=(b, N_HEADS_B),
        in_specs=[
            pl.BlockSpec((None, s, LANES), slot(L0_QB)),
            pl.BlockSpec((None, s, LANES), slot(L0_KB)),
            pl.BlockSpec((None, s, LANES), slot(L0_VB)),
        ],
        out_specs=pl.BlockSpec((None, s, LANES), lambda bi, h: (bi, 0, h)),
        out_shape=jax.ShapeDtypeStruct((b, s, N_HEADS_B * HEAD_DIM), BF16),
        scratch_shapes=[pltpu.VMEM((s, LANES), F32) for _ in range(9)],
        compiler_params=pltpu.CompilerParams(
            dimension_semantics=("parallel", "parallel"), vmem_limit_bytes=VMEM_LIMIT),
        name="dilated",
    )(main3, main3, main3)


def _stick_kernel(q_ref, k_ref, v_ref, o_ref, vt_scr, run_scr, acc_scr, z_scr, *, tq, ck, hg, seq):
    qi = pl.program_id(2)
    nch = ((qi + 1) * tq) // ck
    n_diag = tq // ck

    @pl.when(qi == 0)
    def _():
        _transpose_values(v_ref, vt_scr, hg, seq)

    rr = lax.broadcasted_iota(I32, (ck + 8, 2 * ck), 0)
    cc = lax.broadcasted_iota(I32, (ck + 8, 2 * ck), 1) % ck
    later = jnp.where(jnp.logical_or(cc >= rr, rr >= ck), 1.0, 0.0).astype(BF16)
    dsl = (lax.broadcasted_iota(I32, (ck, tq), 0) - lax.broadcasted_iota(I32, (ck, tq), 1))
    heads = range(hg)
    hs = [slice(h * LANES, (h + 1) * LANES) for h in heads]

    run_scr[...] = jnp.zeros(run_scr.shape, F32)
    acc_scr[...] = jnp.zeros(acc_scr.shape, F32)

    def logits(c):
        start = pl.multiple_of(jnp.maximum(c, 0) * ck, ck)
        return [_dot_nt(k_ref[pl.ds(start, ck), hs[h]], q_ref[:, hs[h]]) for h in heads]

    def chunk(c, zs, masked):
        start = pl.multiple_of(c * ck, ck)
        before = dsl < (qi * tq - c * ck)
        stacked = []
        for h in heads:
            z = zs[h]
            sp = jnp.maximum(z, 0.0) + jnp.log(1.0 + jnp.exp(-jnp.abs(z)))
            if masked:
                sp = jnp.where(before, sp, 0.0)
            hi = pltpu.bitcast(pltpu.bitcast(sp, I32) & np.int32(-65536), F32)
            stacked.append(jnp.concatenate([hi.astype(BF16), (sp - hi).astype(BF16)], axis=0))
        sums = [_dot(later, stacked[h]) for h in heads]
        ws = []
        for h in heads:
            run = run_scr[h]
            arg = zs[h] - (sums[h][:ck] + run[0:1, :])
            if masked:
                w = jnp.where(before, jnp.exp(arg), 0.0).astype(BF16)
            else:
                w = jnp.exp(arg.astype(BF16))
            run_scr[h] = run + sums[h][ck:]
            ws.append(w)
        for h in heads:
            acc_scr[h] += _dot(vt_scr[h, :, pl.ds(start, ck)], ws[h])

    first = nch - 1
    zs = logits(first)
    for d in range(n_diag):
        nxt = logits(first - d - 1)
        chunk(first - d, zs, True)
        zs = nxt
    for h in heads:
        z_scr[0, h] = zs[h]

    def off_diagonal(it, carry):
        c = first - n_diag - it
        slot = it % 2
        zs = [z_scr[slot, h] for h in heads]
        nxt = logits(c - 1)
        chunk(c, zs, False)
        for h in heads:
            z_scr[1 - slot, h] = nxt[h]
        return carry

    lax.fori_loop(0, nch - n_diag, off_diagonal, 0)
    for h in heads:
        o_ref[:, hs[h]] = acc_scr[h].T.astype(BF16)


def _stick(main3, tq, ck, hg):
    b, s, _ = main3.shape
    gw = hg * HEAD_DIM
    assert N_HEADS_C % hg == 0 and tq % ck == 0 and s % tq == 0 and s % 256 == 0
    return pl.pallas_call(
        functools.partial(_stick_kernel, tq=tq, ck=ck, hg=hg, seq=s),
        grid=(b, N_HEADS_C // hg, s // tq),
        in_specs=[
            pl.BlockSpec((None, tq, gw), lambda bi, g, qi: (bi, qi, L1_QC // gw + g)),
            pl.BlockSpec((None, s, gw), lambda bi, g, qi: (bi, 0, L1_KC // gw + g)),
            pl.BlockSpec((None, s, gw), lambda bi, g, qi: (bi, 0, L1_VC // gw + g)),
        ],
        out_specs=pl.BlockSpec((None, tq, gw), lambda bi, g, qi: (bi, qi, g)),
        out_shape=jax.ShapeDtypeStruct((b, s, N_HEADS_C * HEAD_DIM), BF16),
        scratch_shapes=[
            pltpu.VMEM((hg, LANES, s), BF16),
            pltpu.VMEM((hg, 8, tq), F32),
            pltpu.VMEM((hg, LANES, tq), F32),
            pltpu.VMEM((2, hg, ck, tq), F32),
        ],
        compiler_params=pltpu.CompilerParams(
            dimension_semantics=("parallel", "parallel", "arbitrary"), vmem_limit_bytes=VMEM_LIMIT),
        name="stick",
    )(main3, main3, main3)


def _split3(x):
    x1 = x.astype(BF16)
    r1 = x - x1.astype(F32)
    x2 = r1.astype(BF16)
    x3 = (r1 - x2.astype(F32)).astype(BF16)
    return x1, x2, x3


def _forget_cumsum_kernel(fl_ref, b_ref, c_ref, *, heads, rows):
    x = fl_ref[...] + b_ref[...]
    lf = _log_sigmoid(x).reshape(heads * rows, LANES)
    jj = lax.broadcasted_iota(I32, (LANES, 2 * LANES), 0)
    ss = lax.broadcasted_iota(I32, (LANES, 2 * LANES), 1)
    upper = jnp.where(jnp.logical_or(jj <= ss, ss >= LANES), 1.0, 0.0).astype(BF16)
    both = sum(_dot(part, upper) for part in _split3(lf))
    within = both[:, :LANES]
    total = both[:, LANES:]
    n = heads * rows
    aa = lax.broadcasted_iota(I32, (n, n), 0)
    bb = lax.broadcasted_iota(I32, (n, n), 1)
    same_head = (aa // rows) == (bb // rows)
    lower = jnp.where(jnp.logical_and(same_head, bb < aa), 1.0, 0.0).astype(BF16)
    offs = sum(_dot(lower, part) for part in _split3(total))
    c_ref[...] = (within + offs).reshape(heads, rows, LANES)


def _forget_cumsum(fl4, bf3):
    b, heads, rows, _ = fl4.shape
    return pl.pallas_call(
        functools.partial(_forget_cumsum_kernel, heads=heads, rows=rows),
        grid=(b,),
        in_specs=[
            pl.BlockSpec((None, heads, rows, LANES), lambda bi: (bi, 0, 0, 0)),
            pl.BlockSpec((heads, 1, LANES), lambda bi: (0, 0, 0)),
        ],
        out_specs=pl.BlockSpec((None, heads, rows, LANES), lambda bi: (bi, 0, 0, 0)),
        out_shape=jax.ShapeDtypeStruct(fl4.shape, F32),
        compiler_params=pltpu.CompilerParams(dimension_semantics=("parallel",)),
        name="forget_cumsum",
    )(fl4, bf3)


def _fox_kernel(q_ref, k_ref, v_ref, crow_ref, o_ref, vt_scr, cb_scr, m_scr, l_scr, acc_scr, s_scr, p_scr, a_scr,
                *, tq, hg, seq):
    qi = pl.program_id(2)
    heads = range(hg)
    hs = [slice(h * LANES, (h + 1) * LANES) for h in heads]
    reps = tq // LANES

    @pl.when(qi == 0)
    def _():
        _transpose_values(v_ref, vt_scr, hg, seq)

        def body(i, carry):
            start = pl.multiple_of(i * LANES, LANES)
            for h in heads:
                row = crow_ref[h, :, pl.ds(start, LANES)]
                cb_scr[h, pl.ds(start, LANES), :] = jnp.broadcast_to(row, (LANES, LANES)).T
            return carry

        lax.fori_loop(0, seq // LANES, body, 0)

    m_scr[...] = jnp.full(m_scr.shape, NEG_INIT, F32)
    l_scr[...] = jnp.zeros(l_scr.shape, F32)
    acc_scr[...] = jnp.zeros(acc_scr.shape, F32)
    c_t = [crow_ref[h, :, pl.ds(pl.multiple_of(qi * tq, tq), tq)] for h in heads]

    def logits(c):
        start = pl.multiple_of(c * tq, tq)
        return [_dot_nt(k_ref[pl.ds(start, tq), hs[h]], q_ref[:, hs[h]]) for h in heads]

    def softmax_update(c, scores, diagonal):
        start = pl.multiple_of(c * tq, tq)
        probs, alphas = [], []
        for h in heads:
            cb = cb_scr[h, pl.ds(start, tq), :]
            u = scores[h] - (jnp.concatenate([cb] * reps, axis=1) if reps > 1 else cb)
            if diagonal:
                key = lax.broadcasted_iota(I32, (tq, tq), 0)
                qry = lax.broadcasted_iota(I32, (tq, tq), 1)
                u = jnp.where(key <= qry, u, NEG_MASKED)
            m_old = m_scr[h, 0:1, :]
            m_new = jnp.maximum(m_old, jnp.max(u, axis=0, keepdims=True) + c_t[h])
            alphas.append(jnp.exp(m_old - m_new))
            probs.append(jnp.exp((u - (m_new - c_t[h])).astype(BF16)))
            m_scr[h] = jnp.broadcast_to(m_new, (8, tq))
        return probs, alphas

    def accumulate(c, probs, alphas):
        start = pl.multiple_of(jnp.maximum(c, 0) * tq, tq)
        for h in heads:
            pv = _dot(vt_scr[h, :, pl.ds(start, tq)], probs[h])
            acc_scr[h] = alphas[h] * acc_scr[h] + pv[:LANES]
            l_scr[h] = jnp.broadcast_to(alphas[h] * l_scr[h, 0:1, :] + pv[LANES:LANES + 1], (8, tq))

    first = logits(0)
    for h in heads:
        s_scr[h] = first[h]
        p_scr[h] = jnp.zeros((tq, tq), BF16)
        a_scr[h] = jnp.ones((8, tq), F32)

    def off_diagonal(c, carry):
        scores = [s_scr[h] for h in heads]
        pending = ([p_scr[h] for h in heads], [a_scr[h, 0:1, :] for h in heads])
        accumulate(c - 1, *pending)
        nxt = logits(c + 1)
        probs, alphas = softmax_update(c, scores, False)
        for h in heads:
            s_scr[h] = nxt[h]
            p_scr[h] = probs[h]
            a_scr[h] = jnp.broadcast_to(alphas[h], (8, tq))
        return carry

    lax.fori_loop(0, qi, off_diagonal, 0)
    accumulate(qi - 1, [p_scr[h] for h in heads], [a_scr[h, 0:1, :] for h in heads])
    accumulate(qi, *softmax_update(qi, [s_scr[h] for h in heads], True))
    for h in heads:
        o_ref[:, hs[h]] = (acc_scr[h] / l_scr[h, 0:1, :]).T.astype(BF16)


def _fox(main3, crow, tq, hg):
    b, s, _ = main3.shape
    gw = hg * HEAD_DIM
    assert N_HEADS_D % hg == 0 and s % tq == 0 and s % 256 == 0
    return pl.pallas_call(
        functools.partial(_fox_kernel, tq=tq, hg=hg, seq=s),
        grid=(b, N_HEADS_D // hg, s // tq),
        in_specs=[
            pl.BlockSpec((None, tq, gw), lambda bi, g, qi: (bi, qi, L1_QD // gw + g)),
            pl.BlockSpec((None, s, gw), lambda bi, g, qi: (bi, 0, L1_KD // gw + g)),
            pl.BlockSpec((None, s, gw), lambda bi, g, qi: (bi, 0, L1_VD // gw + g)),
            pl.BlockSpec((None, hg, 1, s), lambda bi, g, qi: (bi, g, 0, 0)),
        ],
        out_specs=pl.BlockSpec((None, tq, gw), lambda bi, g, qi: (bi, qi, g)),
        out_shape=jax.ShapeDtypeStruct((b, s, N_HEADS_D * HEAD_DIM), BF16),
        scratch_shapes=[
            pltpu.VMEM((hg, LANES + BF16_ROWS, s), BF16),
            pltpu.VMEM((hg, s, LANES), F32),
            pltpu.VMEM((hg, 8, tq), F32),
            pltpu.VMEM((hg, 8, tq), F32),
            pltpu.VMEM((hg, LANES, tq), F32),
            pltpu.VMEM((hg, tq, tq), F32),
            pltpu.VMEM((hg, tq, tq), BF16),
            pltpu.VMEM((hg, 8, tq), F32),
        ],
        compiler_params=pltpu.CompilerParams(
            dimension_semantics=("parallel", "parallel", "arbitrary"), vmem_limit_bytes=VMEM_LIMIT),
        name="fox",
    )(main3, main3, main3, crow)


def _outproj_kernel(x_ref, ya_ref, yb_ref, gate_ref, w_ref, gf_ref, o_ref, *, na, final_norm):
    g = gate_ref[...].astype(F32)
    silu = g / (1.0 + jnp.exp(-g))
    ya = (ya_ref[...].astype(F32) * silu[:, :na]).astype(BF16)
    yb = (yb_ref[...].astype(F32) * silu[:, na:]).astype(BF16)
    out = x_ref[...] + _dot(ya, w_ref[:na, :]) + _dot(yb, w_ref[na:, :])
    if final_norm:
        ms = jnp.mean(out * out, axis=-1, keepdims=True)
        out = (out * lax.rsqrt(ms + EPS)) * gf_ref[...]
    o_ref[...] = out


def _outproj(x2, ya, yb, main2, w, gf, final_norm, tm):
    m, d = x2.shape
    na, nb = ya.shape[1], yb.shape[1]
    assert na + nb == w.shape[0] and m % tm == 0
    return pl.pallas_call(
        functools.partial(_outproj_kernel, na=na, final_norm=final_norm),
        grid=(m // tm,),
        in_specs=[
            pl.BlockSpec((tm, d), lambda i: (i, 0)),
            pl.BlockSpec((tm, na), lambda i: (i, 0)),
            pl.BlockSpec((tm, nb), lambda i: (i, 0)),
            pl.BlockSpec((tm, na + nb), lambda i: (i, 0)),
            pl.BlockSpec(w.shape, lambda i: (0, 0)),
            pl.BlockSpec((1, d), lambda i: (0, 0)),
        ],
        out_specs=pl.BlockSpec((tm, d), lambda i: (i, 0)),
        out_shape=jax.ShapeDtypeStruct((m, d), F32),
        compiler_params=pltpu.CompilerParams(
            dimension_semantics=("parallel",), vmem_limit_bytes=VMEM_LIMIT),
        name="outproj",
    )(x2, ya, yb, main2, w, gf.reshape(1, d))


def _split_cols(w, sizes):
    offs = np.cumsum(sizes)[:-1].tolist()
    return jnp.split(w, offs, axis=1)


def _layer0_weights(w_in0):
    w_in0 = w_in0.astype(BF16)
    d = w_in0.shape[0]
    a_q, iq_w, b_w = N_HEADS_A * HEAD_DIM, IDX_HEADS * IDX_DIM, N_HEADS_B * HEAD_DIM
    qa, ka, va, iq, ik, iw, qb, kb, vb, gate = _split_cols(
        w_in0, (a_q, HEAD_DIM, HEAD_DIM, iq_w, IDX_DIM, IDX_HEADS, b_w, b_w, b_w, d))
    z = lambda n: jnp.zeros((d, n), w_in0.dtype)
    iklo = jnp.concatenate([ik, iw, z(LANES - IDX_DIM - IDX_HEADS)], axis=1)
    ikhi = jnp.concatenate([z(LANES - IDX_DIM), ik], axis=1)
    w = jnp.concatenate([gate, iq, qb, qa, ka, va, iklo, ikhi, kb, vb], axis=1)
    assert w.shape[1] == L0_WIDTH
    blocks = ([(E_NONE, E_NONE)] * 8 + [(E_IDX64, E_IDX64)] * 4 + [(E_Q128, E_Q128)] * 3
              + [(E_Q128, E_Q128)] * 5 + [(E_K128, E_NONE)] + [(E_IKLO, E_IDX64)]
              + [(E_K128, E_K128)] * 3 + [(E_NONE, E_NONE)] * 3)
    return w.astype(BF16), blocks, L0_IKLO // MXU_COLS


def _layer1_weights(w_in1):
    w_in1 = w_in1.astype(BF16)
    d = w_in1.shape[0]
    c_w, d_w = N_HEADS_C * HEAD_DIM, N_HEADS_D * HEAD_DIM
    qc, kc, vc, qd, kd, vd, fl, gate = _split_cols(w_in1, (c_w, c_w, c_w, d_w, d_w, d_w, N_HEADS_D, d))
    flp = jnp.concatenate([fl, jnp.zeros((d, MXU_COLS - N_HEADS_D), w_in1.dtype)], axis=1)
    w = jnp.concatenate([gate, qc, kc, vc, qd, kd, vd, flp], axis=1)
    assert w.shape[1] == L1_WIDTH
    blocks = ([(E_NONE, E_NONE)] * 8 + [(E_QSCALE, E_QSCALE)] * 4 + [(E_NONE, E_NONE)] * 8
              + [(E_QSCALE, E_QSCALE)] * 4 + [(E_NONE, E_NONE)] * 8 + [(E_NONE, E_NONE)])
    return w.astype(BF16), blocks, L1_FL // MXU_COLS


def kernel(x, norm0, w_in0, w_out0, norm1, w_in1, b_f1, w_out1, norm_f):
    b, s, d = x.shape
    m = b * s
    x2 = x.reshape(m, d)
    tm_in = min(256, s)
    tm_out = min(256, s)

    w0, blocks0, small0 = _layer0_weights(w_in0)
    main0, small0_out = _inproj(x2, norm0, w0, _rope_tables(s), blocks0, small0, s, tm_in)
    main0_3 = main0.reshape(b, s, L0_WIDTH)
    iwt = small0_out[:, IDX_DIM:IDX_DIM + IDX_HEADS].reshape(b, s, IDX_HEADS).transpose(0, 2, 1)
    y_a = _dsa(main0_3, iwt, tq=256)
    y_b = _dilated(main0_3)
    x2 = _outproj(x2, y_a.reshape(m, -1), y_b.reshape(m, -1), main0, w_out0.astype(BF16), norm_f,
                  final_norm=False, tm=tm_out)

    w1, blocks1, small1 = _layer1_weights(w_in1)
    main1, small1_out = _inproj(x2, norm1, w1, None, blocks1, small1, s, tm_in)
    main1_3 = main1.reshape(b, s, L1_WIDTH)
    y_c = _stick(main1_3, tq=256, ck=128, hg=8)
    fl = small1_out[:, :N_HEADS_D].reshape(b, s, N_HEADS_D).transpose(0, 2, 1)
    c = _forget_cumsum(fl.reshape(b, N_HEADS_D, s // LANES, LANES),
                       jnp.broadcast_to(b_f1.astype(F32)[:, None, None], (N_HEADS_D, 1, LANES)))
    crow = c.reshape(b, N_HEADS_D, 1, s)
    y_d = _fox(main1_3, crow, tq=256, hg=4)
    out = _outproj(x2, y_c.reshape(m, -1), y_d.reshape(m, -1), main1, w_out1.astype(BF16), norm_f,
                   final_norm=True, tm=tm_out)
    return out.reshape(b, s, d)
```

```python
import functools

import numpy as np
import jax
import jax.numpy as jnp
from jax import lax
from jax.experimental import pallas as pl
from jax.experimental.pallas import tpu as pltpu

F32 = jnp.float32
BF16 = jnp.bfloat16
I32 = jnp.int32

HEAD_DIM = 128
N_HEADS_A = 10
N_HEADS_B = 6
N_HEADS_C = 8
N_HEADS_D = 8
IDX_HEADS = 16
IDX_DIM = 64
TOPK_MAX = 256
DILATED = ((128, 1), (512, 4), (2048, 16))
ROPE_THETA = 10000.0
EPS = 1e-6

LANES = 128
MXU_COLS = 256
BF16_ROWS = 16
VMEM_LIMIT = 56 * 1024 * 1024

NEG_MASKED = -1.0e30
NEG_INIT = -0.5e30
INT_MIN = np.int32(-2 ** 31)

L0_GATE, L0_IQ, L0_QB, L0_QA = 0, 2048, 3072, 3840
L0_KA, L0_VA, L0_IKLO, L0_IKHI, L0_KB, L0_VB = 5120, 5248, 5376, 5504, 5632, 6400
L0_WIDTH = 7168
L1_GATE, L1_QC, L1_KC, L1_VC, L1_QD, L1_KD, L1_VD, L1_FL = 0, 2048, 3072, 4096, 5120, 6144, 7168, 8192
L1_WIDTH = 8448

E_NONE, E_Q128, E_K128, E_QSCALE, E_IDX64, E_IKLO = range(6)


def _dot_nt(a, b):
    return lax.dot_general(a, b, (((1,), (1,)), ((), ())), preferred_element_type=F32)


def _dot(a, b):
    return jnp.dot(a, b, preferred_element_type=F32)


def _log_sigmoid(z):
    return jnp.minimum(z, 0.0) - jnp.log(1.0 + jnp.exp(-jnp.abs(z)))


def _rope128(t, cos, sin):
    return t * cos + pltpu.roll(t, 64, 1) * sin


def _rope64(t, cos, sin_up, sin_dn):
    return t * cos + pltpu.roll(t, 96, 1) * sin_up + pltpu.roll(t, 32, 1) * sin_dn


def _inproj_kernel(*refs, blocks, small_block, use_rope):
    if use_rope:
        x_ref, g_ref, w_ref, c128, s128, c64, su64, sd64, o_ref, small_ref, h_scr = refs
    else:
        x_ref, g_ref, w_ref, o_ref, small_ref, h_scr = refs
    x = x_ref[...]
    ms = jnp.mean(x * x, axis=-1, keepdims=True)
    h_scr[...] = ((x * lax.rsqrt(ms + EPS)) * g_ref[...]).astype(BF16)
    scale = HEAD_DIM ** -0.5

    def half(a, kind):
        if kind == E_NONE:
            f = a
        elif kind == E_Q128:
            f = _rope128(a * scale, c128[...], s128[...])
        elif kind == E_K128:
            f = _rope128(a, c128[...], s128[...])
        elif kind == E_QSCALE:
            f = a * scale
        elif kind == E_IDX64:
            f = _rope64(a, c64[...], su64[...], sd64[...])
        elif kind == E_IKLO:
            lane = lax.broadcasted_iota(I32, a.shape, 1)
            r = _rope64(a, c64[...], su64[...], sd64[...])
            f = jnp.where(lane < IDX_DIM, r, a)
            return f, jnp.where(lane < IDX_DIM, r, 0.0).astype(BF16)
        else:
            raise ValueError(kind)
        return f, f.astype(BF16)

    for jj, kinds in enumerate(blocks):
        c0 = jj * MXU_COLS
        acc = _dot(h_scr[...], w_ref[:, c0:c0 + MXU_COLS])
        f0, b0 = half(acc[:, :LANES], kinds[0])
        f1, b1 = half(acc[:, LANES:], kinds[1])
        o_ref[:, c0:c0 + LANES] = b0
        o_ref[:, c0 + LANES:c0 + MXU_COLS] = b1
        if jj == small_block:
            small_ref[:, :LANES] = f0
            small_ref[:, LANES:] = f1


def _inproj(x2, g, w, tables, blocks, small_block, seq, tm):
    m, d = x2.shape
    n = w.shape[1]
    assert n == len(blocks) * MXU_COLS and m % tm == 0 and seq % tm == 0
    use_rope = tables is not None
    per_seq = seq // tm
    in_specs = [
        pl.BlockSpec((tm, d), lambda i: (i, 0)),
        pl.BlockSpec((1, d), lambda i: (0, 0)),
        pl.BlockSpec((d, n), lambda i: (0, 0), pipeline_mode=pl.Buffered(1)),
    ]
    args = [x2, g.reshape(1, d), w]
    if use_rope:
        for t in tables:
            in_specs.append(pl.BlockSpec((tm, LANES), lambda i: (i % per_seq, 0)))
            args.append(t)
    return pl.pallas_call(
        functools.partial(_inproj_kernel, blocks=tuple(blocks), small_block=small_block, use_rope=use_rope),
        grid=(m // tm,),
        in_specs=in_specs,
        out_specs=[
            pl.BlockSpec((tm, n), lambda i: (i, 0)),
            pl.BlockSpec((tm, MXU_COLS), lambda i: (i, 0)),
        ],
        out_shape=[
            jax.ShapeDtypeStruct((m, n), BF16),
            jax.ShapeDtypeStruct((m, MXU_COLS), F32),
        ],
        scratch_shapes=[pltpu.VMEM((tm, d), BF16)],
        compiler_params=pltpu.CompilerParams(
            dimension_semantics=("parallel",), vmem_limit_bytes=VMEM_LIMIT),
        name="inproj",
    )(*args)


def _rope_tables(seq):
    pos = jnp.arange(seq, dtype=F32)[:, None]
    half = HEAD_DIM // 2
    inv = ROPE_THETA ** (-jnp.arange(half, dtype=F32) / half)
    ang = pos * inv[None, :]
    c128 = jnp.concatenate([jnp.cos(ang), jnp.cos(ang)], axis=1)
    s128 = jnp.concatenate([-jnp.sin(ang), jnp.sin(ang)], axis=1)
    half = IDX_DIM // 2
    inv = ROPE_THETA ** (-jnp.arange(half, dtype=F32) / half)
    ang = pos * inv[None, :]
    zero = jnp.zeros_like(ang)
    c64 = jnp.concatenate([jnp.cos(ang)] * 4, axis=1)
    su64 = jnp.concatenate([-jnp.sin(ang), zero] * 2, axis=1)
    sd64 = jnp.concatenate([zero, jnp.sin(ang)] * 2, axis=1)
    return c128, s128, c64, su64, sd64


def _transpose_values(v_ref, vt_scr, hg, seq):
    piece = 256
    eye = (lax.broadcasted_iota(I32, (LANES, LANES), 0) == lax.broadcasted_iota(I32, (LANES, LANES), 1))
    eye = jnp.where(eye, 1.0, 0.0).astype(BF16)

    extra = vt_scr.shape[1] - LANES

    def body(i, carry):
        start = pl.multiple_of(i * piece, piece)
        for h in range(hg):
            vt = _dot_nt(eye, v_ref[pl.ds(start, piece), h * LANES:(h + 1) * LANES])
            vt_scr[h, 0:LANES, pl.ds(start, piece)] = vt.astype(BF16)
            if extra:
                vt_scr[h, LANES:LANES + extra, pl.ds(start, piece)] = jnp.ones((extra, piece), BF16)
        return carry

    lax.fori_loop(0, seq // piece, body, 0)


def _dsa_kernel(iq_ref, iwt_ref, iklo_ref, ikhi_ref, qa_ref, ka_ref, va_ref, o_ref,
                key_scr, vt_scr, m_scr, l_scr, acc_scr, p_scr, a_scr, *, tq, kc, topk, seq):
    qi = pl.program_id(1)
    nch = ((qi + 1) * tq + kc - 1) // kc
    sub = kc // tq
    key_i = lax.broadcasted_iota(I32, (tq, tq), 0)
    qry_i = lax.broadcasted_iota(I32, (tq, tq), 1)

    @pl.when(qi == 0)
    def _():
        _transpose_values(va_ref, vt_scr, 1, seq)

    wscale = (IDX_DIM ** -0.5) * (IDX_HEADS ** -0.5)
    w = iwt_ref[...] * wscale

    def score_chunk(c, carry):
        start = pl.multiple_of(c * kc, kc)
        klo = iklo_ref[pl.ds(start, kc), :]
        khi = ikhi_ref[pl.ds(start, kc), :]
        score = [jnp.zeros((tq, tq), F32) for _ in range(sub)]
        for p in range(IDX_HEADS // 2):
            iq2 = iq_ref[:, p * LANES:(p + 1) * LANES]
            d0 = _dot_nt(klo, iq2)
            d1 = _dot_nt(khi, iq2)
            w0 = w[2 * p:2 * p + 1, :]
            w1 = w[2 * p + 1:2 * p + 2, :]
            for j in range(sub):
                rows = slice(j * tq, (j + 1) * tq)
                score[j] = score[j] + w0 * jnp.maximum(d0[rows], 0.0) + w1 * jnp.maximum(d1[rows], 0.0)
        for j in range(sub):
            bits = pltpu.bitcast(score[j], I32)
            okey = jnp.where(bits < 0, bits ^ np.int32(0x7FFFFFFF), bits)
            okey = jnp.where(key_i + (c * kc + (j - qi) * tq) <= qry_i, okey, INT_MIN)
            key_scr[pl.ds(start + j * tq, tq), :] = okey
        return carry

    lax.fori_loop(0, nch, score_chunk, 0)

    def bit_step(it, state):
        t_u, n_sel = state
        bit = jnp.left_shift(np.int32(1), 31 - it)
        cand_u = t_u | bit
        cand_s = cand_u ^ INT_MIN

        def count_chunk(c, cnt):
            start = pl.multiple_of(c * kc, kc)
            for j in range(kc // 64):
                k = key_scr[pl.ds(start + j * 64, 64), :]
                cnt = cnt + jnp.where(k >= cand_s, 1.0, 0.0)
            return cnt

        cnt = lax.fori_loop(0, nch, count_chunk, jnp.zeros((64, tq), F32))
        total = jnp.sum(cnt, axis=0, keepdims=True)
        accept = total >= float(topk)
        return jnp.where(accept, cand_u, t_u), jnp.where(accept, total, n_sel)

    t_u, n_sel = lax.fori_loop(0, 32, bit_step, (jnp.zeros((1, tq), I32), jnp.zeros((1, tq), F32)))
    thr = jnp.maximum(t_u ^ INT_MIN, INT_MIN + np.int32(1))

    @pl.when(jnp.max(n_sel) > float(topk))
    def _():
        def count_greater(c, cnt):
            k = key_scr[pl.ds(pl.multiple_of(c * tq, tq), tq), :]
            return cnt + jnp.sum(jnp.where(k > thr, 1.0, 0.0), axis=0, keepdims=True)

        n_gt = lax.fori_loop(0, nch * sub, count_greater, jnp.zeros((1, tq), F32))
        keep = jnp.where(n_sel > float(topk), float(topk) - n_gt, float(seq))
        tri = jnp.where(key_i >= qry_i, 1.0, 0.0).astype(BF16)

        def drop_late_ties(c, seen):
            start = pl.multiple_of(c * tq, tq)
            k = key_scr[pl.ds(start, tq), :]
            tied = k == thr
            rank = _dot(tri, jnp.where(tied, 1.0, 0.0).astype(BF16)) + seen
            key_scr[pl.ds(start, tq), :] = jnp.where(jnp.logical_and(tied, rank > keep), thr - 1, k)
            return rank[tq - 1:tq, :]

        lax.fori_loop(0, nch * sub, drop_late_ties, jnp.zeros((1, tq), F32))

    m_scr[...] = jnp.full(m_scr.shape, NEG_INIT, F32)
    l_scr[...] = jnp.zeros(l_scr.shape, F32)
    acc_scr[...] = jnp.zeros(acc_scr.shape, F32)

    heads = range(N_HEADS_A)

    def keys_of(c):
        return ka_ref[pl.ds(pl.multiple_of(jnp.minimum(c, nch - 1) * kc, kc), kc), :]

    def logits(kk, h):
        return _dot_nt(kk, qa_ref[:, h * LANES:(h + 1) * LANES])

    def values_of(c):
        return vt_scr[0, :, pl.ds(pl.multiple_of(jnp.maximum(c, 0) * kc, kc), kc)]

    def accumulate(vt, h):
        alpha = a_scr[h, 0:1, :]
        pv = _dot(vt, p_scr[h])
        acc_scr[h] = alpha * acc_scr[h] + pv[:LANES]
        l_scr[h] = jnp.broadcast_to(alpha * l_scr[h, 0:1, :] + pv[LANES:LANES + 1], (8, tq))

    for h in heads:
        p_scr[h] = jnp.zeros((kc, tq), BF16)
        a_scr[h] = jnp.ones((8, tq), F32)

    def attn_chunk(c, carry):
        sel = key_scr[pl.ds(pl.multiple_of(c * kc, kc), kc), :] >= thr
        vt = values_of(c - 1)
        kk = keys_of(c)
        for h in heads:
            scores = logits(kk, h)
            accumulate(vt, h)
            sm = jnp.where(sel, scores, NEG_MASKED)
            m_old = m_scr[h, 0:1, :]
            m_new = jnp.maximum(m_old, jnp.max(sm, axis=0, keepdims=True))
            a_scr[h] = jnp.broadcast_to(jnp.exp(m_old - m_new), (8, tq))
            p_scr[h] = jnp.exp((sm - m_new).astype(BF16))
            m_scr[h] = jnp.broadcast_to(m_new, (8, tq))
        return carry

    lax.fori_loop(0, nch, attn_chunk, 0)
    vt = values_of(nch - 1)
    for h in heads:
        accumulate(vt, h)
    for h in range(N_HEADS_A):
        o_ref[:, h * LANES:(h + 1) * LANES] = (acc_scr[h] / l_scr[h, 0:1, :]).T.astype(BF16)


def _dsa(main3, iwt, tq, kc):
    b, s, _ = main3.shape
    topk = min(TOPK_MAX, s // 4)
    a_q = N_HEADS_A * HEAD_DIM
    iq_w = IDX_HEADS * IDX_DIM
    assert s % tq == 0 and s % kc == 0 and kc % tq == 0 and kc % 64 == 0 and s % 256 == 0

    def slot(off):
        return lambda bi, qi: (bi, 0, off // LANES)

    return pl.pallas_call(
        functools.partial(_dsa_kernel, tq=tq, kc=kc, topk=topk, seq=s),
        grid=(b, s // tq),
        in_specs=[
            pl.BlockSpec((None, tq, iq_w), lambda bi, qi: (bi, qi, L0_IQ // iq_w)),
            pl.BlockSpec((None, IDX_HEADS, tq), lambda bi, qi: (bi, 0, qi)),
            pl.BlockSpec((None, s, LANES), slot(L0_IKLO)),
            pl.BlockSpec((None, s, LANES), slot(L0_IKHI)),
            pl.BlockSpec((None, tq, a_q), lambda bi, qi: (bi, qi, L0_QA // a_q)),
            pl.BlockSpec((None, s, LANES), slot(L0_KA)),
            pl.BlockSpec((None, s, LANES), slot(L0_VA)),
        ],
        out_specs=pl.BlockSpec((None, tq, a_q), lambda bi, qi: (bi, qi, 0)),
        out_shape=jax.ShapeDtypeStruct((b, s, a_q), BF16),
        scratch_shapes=[
            pltpu.VMEM((s, tq), I32),
            pltpu.VMEM((1, LANES + BF16_ROWS, s), BF16),
            pltpu.VMEM((N_HEADS_A, 8, tq), F32),
            pltpu.VMEM((N_HEADS_A, 8, tq), F32),
            pltpu.VMEM((N_HEADS_A, LANES, tq), F32),
            pltpu.VMEM((N_HEADS_A, kc, tq), BF16),
            pltpu.VMEM((N_HEADS_A, 8, tq), F32),
        ],
        compiler_params=pltpu.CompilerParams(
            dimension_semantics=("parallel", "arbitrary"), vmem_limit_bytes=VMEM_LIMIT),
        name="dsa",
    )(main3, iwt, main3, main3, main3, main3, main3)


def _dilated_kernel(q_ref, k_ref, v_ref, o_ref, qf, kf, vf, o0, o1, o2, e0, e1, e2, *, seq, blk, unroll):
    qf[...] = q_ref[...].astype(F32)
    kf[...] = k_ref[...].astype(F32)
    vf[...] = v_ref[...].astype(F32)
    o_scr = (o0, o1, o2)
    e_scr = (e0, e1, e2)
    row = lax.broadcasted_iota(I32, (blk, blk), 0)
    col = lax.broadcasted_iota(I32, (blk, blk), 1)

    for pat, (window, dil) in enumerate(DILATED):
        steps = window // dil
        length = seq // dil
        nb = length // blk
        cur_ok = jnp.logical_and(row - col >= 0, row - col <= steps)
        prev_band = jnp.logical_and(row + blk - col >= 0, row + blk - col <= steps)

        def rows(ref, n, r, dil=dil):
            start = n * (blk * dil) + r
            if dil == 1:
                return ref[pl.ds(start, blk), :]
            return ref[pl.ds(start, blk, stride=dil), :]

        def step(it, carry, pat=pat, dil=dil, nb=nb, cur_ok=cur_ok, prev_band=prev_band, rows=rows):
            ids = [it * unroll + u for u in range(unroll)]
            rn = [(i // nb, i % nb) for i in ids]
            ops = []
            for r, n in rn:
                n_prev = jnp.maximum(n - 1, 0)
                ops.append((rows(qf, n, r).astype(BF16), rows(kf, n, r).astype(BF16),
                            rows(kf, n_prev, r).astype(BF16), rows(vf, n, r).astype(BF16),
                            rows(vf, n_prev, r).astype(BF16)))
            scores = [(_dot_nt(q, k_cur), _dot_nt(q, k_prev)) for q, k_cur, k_prev, _, _ in ops]
            probs, stats = [], []
            for (r, n), (sc, sp) in zip(rn, scores):
                s_cur = jnp.where(cur_ok, sc, NEG_MASKED)
                s_prev = jnp.where(jnp.logical_and(prev_band, n > 0), sp, NEG_MASKED)
                m = jnp.max(jnp.maximum(s_cur, s_prev), axis=1, keepdims=True)
                p_cur = jnp.exp(s_cur - m)
                p_prev = jnp.exp(s_prev - m)
                l = jnp.sum(p_cur + p_prev, axis=1, keepdims=True)
                probs.append((p_cur.astype(BF16), p_prev.astype(BF16)))
                stats.append((m, l))
            for (r, n), (p_cur, p_prev), (m, l), (_, _, _, v_cur, v_prev) in zip(rn, probs, stats, ops):
                o = (_dot(p_cur, v_cur) + _dot(p_prev, v_prev)) / l
                lse = jnp.broadcast_to(m + jnp.log(l), (blk, LANES))
                start = n * (blk * dil) + r
                if dil == 1:
                    o_scr[pat][pl.ds(start, blk), :] = o
                    e_scr[pat][pl.ds(start, blk), :] = lse
                else:
                    o_scr[pat][pl.ds(start, blk, stride=dil), :] = o
                    e_scr[pat][pl.ds(start, blk, stride=dil), :] = lse
            return carry

        assert (dil * nb) % unroll == 0
        lax.fori_loop(0, dil * nb // unroll, step, 0)

    rows_per = 512
    def merge(i, carry):
        sl = pl.ds(pl.multiple_of(i * rows_per, rows_per), rows_per)
        es = [e[sl, :] for e in e_scr]
        mx = jnp.maximum(jnp.maximum(es[0], es[1]), es[2])
        ws = [jnp.exp(e - mx) for e in es]
        num = ws[0] * o_scr[0][sl, :] + ws[1] * o_scr[1][sl, :] + ws[2] * o_scr[2][sl, :]
        o_ref[sl, :] = (num / (ws[0] + ws[1] + ws[2])).astype(BF16)
        return carry

    lax.fori_loop(0, seq // rows_per, merge, 0)


def _dilated(main3):
    b, s, _ = main3.shape
    blk = 128
    max_dil = max(d for _, d in DILATED)
    assert s % (blk * max_dil) == 0 and s % 512 == 0

    def slot(off):
        return lambda bi, h: (bi, 0, off // LANES + h)

    return pl.pallas_call(
        functools.partial(_dilated_kernel, seq=s, blk=blk, unroll=s // blk),
        grid=(b, N_HEADS_B),
        in_specs=[
            pl.BlockSpec((None, s, LANES), slot(L0_QB)),
            pl.BlockSpec((None, s, LANES), slot(L0_KB)),
            pl.BlockSpec((None, s, LANES), slot(L0_VB)),
        ],
        out_specs=pl.BlockSpec((None, s, LANES), lambda bi, h: (bi, 0, h)),
        out_shape=jax.ShapeDtypeStruct((b, s, N_HEADS_B * HEAD_DIM), BF16),
        scratch_shapes=[pltpu.VMEM((s, LANES), F32) for _ in range(9)],
        compiler_params=pltpu.CompilerParams(
            dimension_semantics=("parallel", "parallel"), vmem_limit_bytes=VMEM_LIMIT),
        name="dilated",
    )(main3, main3, main3)


def _stick_kernel(q_ref, k_ref, v_ref, o_ref, vt_scr, run_scr, acc_scr, w_scr, *, tq, ck, grp, hg, seq):
    qi = pl.program_id(2)
    nch = ((qi + 1) * tq) // ck
    n_diag = tq // ck

    @pl.when(qi == 0)
    def _():
        _transpose_values(v_ref, vt_scr, hg, seq)

    rr = lax.broadcasted_iota(I32, (ck + 8, 2 * ck), 0)
    cc = lax.broadcasted_iota(I32, (ck + 8, 2 * ck), 1) % ck
    later = jnp.where(jnp.logical_or(cc >= rr, rr >= ck), 1.0, 0.0).astype(BF16)
    dsl = (lax.broadcasted_iota(I32, (ck, tq), 0) - lax.broadcasted_iota(I32, (ck, tq), 1))
    heads = range(hg)
    hs = [slice(h * LANES, (h + 1) * LANES) for h in heads]

    run_scr[...] = jnp.zeros(run_scr.shape, F32)
    acc_scr[...] = jnp.zeros(acc_scr.shape, F32)

    def logits(c):
        start = pl.multiple_of(jnp.maximum(c, 0) * ck, ck)
        return [_dot_nt(k_ref[pl.ds(start, ck), hs[h]], q_ref[:, hs[h]]) for h in heads]

    def weights(c, zs, masked):
        before = dsl < (qi * tq - c * ck)
        stacked = []
        for h in heads:
            z = zs[h]
            sp = jnp.maximum(z, 0.0) + jnp.log(1.0 + jnp.exp(-jnp.abs(z)))
            if masked:
                sp = jnp.where(before, sp, 0.0)
            hi = pltpu.bitcast(pltpu.bitcast(sp, I32) & np.int32(-65536), F32)
            stacked.append(jnp.concatenate([hi.astype(BF16), (sp - hi).astype(BF16)], axis=0))
        sums = [_dot(later, stacked[h]) for h in heads]
        ws = []
        for h in heads:
            run = run_scr[h]
            arg = zs[h] - (sums[h][:ck] + run[0:1, :])
            if masked:
                w = jnp.where(before, jnp.exp(arg), 0.0).astype(BF16)
            else:
                w = jnp.exp(arg.astype(BF16))
            run_scr[h] = run + sums[h][ck:]
            ws.append(w)
        return ws

    def accumulate(c_lo, ws, n):
        start = pl.multiple_of(c_lo * ck, ck)
        for h in heads:
            acc_scr[h] += _dot(vt_scr[h, :, pl.ds(start, n * ck)], ws[h])

    def group_weights(c_hi, masked):
        zs = [logits(c_hi - j) for j in range(grp)]
        ws = [weights(c_hi - j, zs[j], masked) for j in range(grp)]
        return [jnp.concatenate([ws[j][h] for j in reversed(range(grp))], axis=0) for h in heads]

    assert n_diag % grp == 0
    first = nch - 1
    for d in range(0, n_diag, grp):
        accumulate(first - d - (grp - 1), group_weights(first - d, True), grp)

    for h in heads:
        w_scr[h] = jnp.zeros((grp * ck, tq), BF16)

    def off_diagonal(it, carry):
        c_hi = first - n_diag - grp * it
        accumulate(jnp.minimum(c_hi + 1, first - (grp - 1)), [w_scr[h] for h in heads], grp)
        ws = group_weights(c_hi, False)
        for h in heads:
            w_scr[h] = ws[h]
        return carry

    lax.fori_loop(0, (nch - n_diag) // grp, off_diagonal, 0)
    accumulate(0, [w_scr[h] for h in heads], grp)
    for h in heads:
        o_ref[:, hs[h]] = acc_scr[h].T.astype(BF16)


def _stick(main3, tq, ck, grp, hg):
    b, s, _ = main3.shape
    gw = hg * HEAD_DIM
    assert N_HEADS_C % hg == 0 and tq % (ck * grp) == 0 and s % tq == 0 and s % 256 == 0
    return pl.pallas_call(
        functools.partial(_stick_kernel, tq=tq, ck=ck, grp=grp, hg=hg, seq=s),
        grid=(b, N_HEADS_C // hg, s // tq),
        in_specs=[
            pl.BlockSpec((None, tq, gw), lambda bi, g, qi: (bi, qi, L1_QC // gw + g)),
            pl.BlockSpec((None, s, gw), lambda bi, g, qi: (bi, 0, L1_KC // gw + g)),
            pl.BlockSpec((None, s, gw), lambda bi, g, qi: (bi, 0, L1_VC // gw + g)),
        ],
        out_specs=pl.BlockSpec((None, tq, gw), lambda bi, g, qi: (bi, qi, g)),
        out_shape=jax.ShapeDtypeStruct((b, s, N_HEADS_C * HEAD_DIM), BF16),
        scratch_shapes=[
            pltpu.VMEM((hg, LANES, s), BF16),
            pltpu.VMEM((hg, 8, tq), F32),
            pltpu.VMEM((hg, LANES, tq), F32),
            pltpu.VMEM((hg, grp * ck, tq), BF16),
        ],
        compiler_params=pltpu.CompilerParams(
            dimension_semantics=("parallel", "parallel", "arbitrary"), vmem_limit_bytes=VMEM_LIMIT),
        name="stick",
    )(main3, main3, main3)


def _split3(x):
    x1 = x.astype(BF16)
    r1 = x - x1.astype(F32)
    x2 = r1.astype(BF16)
    x3 = (r1 - x2.astype(F32)).astype(BF16)
    return x1, x2, x3


def _forget_cumsum_kernel(fl_ref, b_ref, c_ref, *, heads, rows):
    x = fl_ref[...] + b_ref[...]
    lf = _log_sigmoid(x).reshape(heads * rows, LANES)
    jj = lax.broadcasted_iota(I32, (LANES, 2 * LANES), 0)
    ss = lax.broadcasted_iota(I32, (LANES, 2 * LANES), 1)
    upper = jnp.where(jnp.logical_or(jj <= ss, ss >= LANES), 1.0, 0.0).astype(BF16)
    both = sum(_dot(part, upper) for part in _split3(lf))
    within = both[:, :LANES]
    total = both[:, LANES:]
    n = heads * rows
    aa = lax.broadcasted_iota(I32, (n, n), 0)
    bb = lax.broadcasted_iota(I32, (n, n), 1)
    same_head = (aa // rows) == (bb // rows)
    lower = jnp.where(jnp.logical_and(same_head, bb < aa), 1.0, 0.0).astype(BF16)
    offs = sum(_dot(lower, part) for part in _split3(total))
    c_ref[...] = (within + offs).reshape(heads, rows, LANES)


def _forget_cumsum(fl4, bf3):
    b, heads, rows, _ = fl4.shape
    return pl.pallas_call(
        functools.partial(_forget_cumsum_kernel, heads=heads, rows=rows),
        grid=(b,),
        in_specs=[
            pl.BlockSpec((None, heads, rows, LANES), lambda bi: (bi, 0, 0, 0)),
            pl.BlockSpec((heads, 1, LANES), lambda bi: (0, 0, 0)),
        ],
        out_specs=pl.BlockSpec((None, heads, rows, LANES), lambda bi: (bi, 0, 0, 0)),
        out_shape=jax.ShapeDtypeStruct(fl4.shape, F32),
        compiler_params=pltpu.CompilerParams(dimension_semantics=("parallel",)),
        name="forget_cumsum",
    )(fl4, bf3)


def _fox_kernel(q_ref, k_ref, v_ref, crow_ref, o_ref, vt_scr, cb_scr, m_scr, l_scr, acc_scr, p_scr, a_scr,
                *, tq, kc, hg, seq):
    qi = pl.program_id(2)
    heads = range(hg)
    hs = [slice(h * LANES, (h + 1) * LANES) for h in heads]
    reps = tq // LANES

    @pl.when(qi == 0)
    def _():
        _transpose_values(v_ref, vt_scr, hg, seq)

        def body(i, carry):
            start = pl.multiple_of(i * LANES, LANES)
            for h in heads:
                row = crow_ref[h, :, pl.ds(start, LANES)]
                cb_scr[h, pl.ds(start, LANES), :] = jnp.broadcast_to(row, (LANES, LANES)).T
            return carry

        lax.fori_loop(0, seq // LANES, body, 0)

    m_scr[...] = jnp.full(m_scr.shape, NEG_INIT, F32)
    l_scr[...] = jnp.zeros(l_scr.shape, F32)
    acc_scr[...] = jnp.zeros(acc_scr.shape, F32)
    c_t = [crow_ref[h, :, pl.ds(pl.multiple_of(qi * tq, tq), tq)] for h in heads]

    last = ((qi + 1) * tq - 1) // kc

    def logits(c):
        start = pl.multiple_of(c * kc, kc)
        return [_dot_nt(k_ref[pl.ds(start, kc), hs[h]], q_ref[:, hs[h]]) for h in heads]

    def softmax_update(c, scores, diagonal):
        start = pl.multiple_of(c * kc, kc)
        probs, alphas = [], []
        for h in heads:
            cb = cb_scr[h, pl.ds(start, kc), :]
            u = scores[h] - (jnp.concatenate([cb] * reps, axis=1) if reps > 1 else cb)
            if diagonal:
                key = lax.broadcasted_iota(I32, (kc, tq), 0) + (c * kc - qi * tq)
                qry = lax.broadcasted_iota(I32, (kc, tq), 1)
                u = jnp.where(key <= qry, u, NEG_MASKED)
            m_old = m_scr[h, 0:1, :]
            m_new = jnp.maximum(m_old, jnp.max(u, axis=0, keepdims=True) + c_t[h])
            alphas.append(jnp.exp(m_old - m_new))
            probs.append(jnp.exp((u - (m_new - c_t[h])).astype(BF16)))
            m_scr[h] = jnp.broadcast_to(m_new, (8, tq))
        return probs, alphas

    def accumulate(c, probs, alphas):
        start = pl.multiple_of(jnp.maximum(c, 0) * kc, kc)
        for h in heads:
            pv = _dot(vt_scr[h, :, pl.ds(start, kc)], probs[h])
            acc_scr[h] = alphas[h] * acc_scr[h] + pv[:LANES]
            l_scr[h] = jnp.broadcast_to(alphas[h] * l_scr[h, 0:1, :] + pv[LANES:LANES + 1], (8, tq))

    for h in heads:
        p_scr[h] = jnp.zeros((kc, tq), BF16)
        a_scr[h] = jnp.ones((8, tq), F32)

    def off_diagonal(c, carry):
        scores = logits(c)
        pending = ([p_scr[h] for h in heads], [a_scr[h, 0:1, :] for h in heads])
        accumulate(c - 1, *pending)
        probs, alphas = softmax_update(c, scores, False)
        for h in heads:
            p_scr[h] = probs[h]
            a_scr[h] = jnp.broadcast_to(alphas[h], (8, tq))
        return carry

    lax.fori_loop(0, last, off_diagonal, 0)
    accumulate(last - 1, [p_scr[h] for h in heads], [a_scr[h, 0:1, :] for h in heads])
    accumulate(last, *softmax_update(last, logits(last), True))
    for h in heads:
        o_ref[:, hs[h]] = (acc_scr[h] / l_scr[h, 0:1, :]).T.astype(BF16)


def _fox(main3, crow, tq, kc, hg):
    b, s, _ = main3.shape
    gw = hg * HEAD_DIM
    assert N_HEADS_D % hg == 0 and s % tq == 0 and s % kc == 0 and kc % tq == 0 and s % 256 == 0
    return pl.pallas_call(
        functools.partial(_fox_kernel, tq=tq, kc=kc, hg=hg, seq=s),
        grid=(b, N_HEADS_D // hg, s // tq),
        in_specs=[
            pl.BlockSpec((None, tq, gw), lambda bi, g, qi: (bi, qi, L1_QD // gw + g)),
            pl.BlockSpec((None, s, gw), lambda bi, g, qi: (bi, 0, L1_KD // gw + g)),
            pl.BlockSpec((None, s, gw), lambda bi, g, qi: (bi, 0, L1_VD // gw + g)),
            pl.BlockSpec((None, hg, 1, s), lambda bi, g, qi: (bi, g, 0, 0)),
        ],
        out_specs=pl.BlockSpec((None, tq, gw), lambda bi, g, qi: (bi, qi, g)),
        out_shape=jax.ShapeDtypeStruct((b, s, N_HEADS_D * HEAD_DIM), BF16),
        scratch_shapes=[
            pltpu.VMEM((hg, LANES + BF16_ROWS, s), BF16),
            pltpu.VMEM((hg, s, LANES), F32),
            pltpu.VMEM((hg, 8, tq), F32),
            pltpu.VMEM((hg, 8, tq), F32),
            pltpu.VMEM((hg, LANES, tq), F32),
            pltpu.VMEM((hg, kc, tq), BF16),
            pltpu.VMEM((hg, 8, tq), F32),
        ],
        compiler_params=pltpu.CompilerParams(
            dimension_semantics=("parallel", "parallel", "arbitrary"), vmem_limit_bytes=VMEM_LIMIT),
        name="fox",
    )(main3, main3, main3, crow)


def _outproj_kernel(x_ref, ya_ref, yb_ref, gate_ref, w_ref, gf_ref, o_ref, *, na, final_norm):
    g = gate_ref[...].astype(F32)
    silu = g / (1.0 + jnp.exp(-g))
    ya = (ya_ref[...].astype(F32) * silu[:, :na]).astype(BF16)
    yb = (yb_ref[...].astype(F32) * silu[:, na:]).astype(BF16)
    out = x_ref[...] + _dot(ya, w_ref[:na, :]) + _dot(yb, w_ref[na:, :])
    if final_norm:
        ms = jnp.mean(out * out, axis=-1, keepdims=True)
        out = (out * lax.rsqrt(ms + EPS)) * gf_ref[...]
    o_ref[...] = out


def _outproj(x2, ya, yb, main2, w, gf, final_norm, tm):
    m, d = x2.shape
    na, nb = ya.shape[1], yb.shape[1]
    assert na + nb == w.shape[0] and m % tm == 0
    return pl.pallas_call(
        functools.partial(_outproj_kernel, na=na, final_norm=final_norm),
        grid=(m // tm,),
        in_specs=[
            pl.BlockSpec((tm, d), lambda i: (i, 0)),
            pl.BlockSpec((tm, na), lambda i: (i, 0)),
            pl.BlockSpec((tm, nb), lambda i: (i, 0)),
            pl.BlockSpec((tm, na + nb), lambda i: (i, 0)),
            pl.BlockSpec(w.shape, lambda i: (0, 0)),
            pl.BlockSpec((1, d), lambda i: (0, 0)),
        ],
        out_specs=pl.BlockSpec((tm, d), lambda i: (i, 0)),
        out_shape=jax.ShapeDtypeStruct((m, d), F32),
        compiler_params=pltpu.CompilerParams(
            dimension_semantics=("parallel",), vmem_limit_bytes=VMEM_LIMIT),
        name="outproj",
    )(x2, ya, yb, main2, w, gf.reshape(1, d))


def _split_cols(w, sizes):
    offs = np.cumsum(sizes)[:-1].tolist()
    return jnp.split(w, offs, axis=1)


def _layer0_weights(w_in0):
    w_in0 = w_in0.astype(BF16)
    d = w_in0.shape[0]
    a_q, iq_w, b_w = N_HEADS_A * HEAD_DIM, IDX_HEADS * IDX_DIM, N_HEADS_B * HEAD_DIM
    qa, ka, va, iq, ik, iw, qb, kb, vb, gate = _split_cols(
        w_in0, (a_q, HEAD_DIM, HEAD_DIM, iq_w, IDX_DIM, IDX_HEADS, b_w, b_w, b_w, d))
    z = lambda n: jnp.zeros((d, n), w_in0.dtype)
    iklo = jnp.concatenate([ik, iw, z(LANES - IDX_DIM - IDX_HEADS)], axis=1)
    ikhi = jnp.concatenate([z(LANES - IDX_DIM), ik], axis=1)
    w = jnp.concatenate([gate, iq, qb, qa, ka, va, iklo, ikhi, kb, vb], axis=1)
    assert w.shape[1] == L0_WIDTH
    blocks = ([(E_NONE, E_NONE)] * 8 + [(E_IDX64, E_IDX64)] * 4 + [(E_Q128, E_Q128)] * 3
              + [(E_Q128, E_Q128)] * 5 + [(E_K128, E_NONE)] + [(E_IKLO, E_IDX64)]
              + [(E_K128, E_K128)] * 3 + [(E_NONE, E_NONE)] * 3)
    return w.astype(BF16), blocks, L0_IKLO // MXU_COLS


def _layer1_weights(w_in1):
    w_in1 = w_in1.astype(BF16)
    d = w_in1.shape[0]
    c_w, d_w = N_HEADS_C * HEAD_DIM, N_HEADS_D * HEAD_DIM
    qc, kc, vc, qd, kd, vd, fl, gate = _split_cols(w_in1, (c_w, c_w, c_w, d_w, d_w, d_w, N_HEADS_D, d))
    flp = jnp.concatenate([fl, jnp.zeros((d, MXU_COLS - N_HEADS_D), w_in1.dtype)], axis=1)
    w = jnp.concatenate([gate, qc, kc, vc, qd, kd, vd, flp], axis=1)
    assert w.shape[1] == L1_WIDTH
    blocks = ([(E_NONE, E_NONE)] * 8 + [(E_QSCALE, E_QSCALE)] * 4 + [(E_NONE, E_NONE)] * 8
              + [(E_QSCALE, E_QSCALE)] * 4 + [(E_NONE, E_NONE)] * 8 + [(E_NONE, E_NONE)])
    return w.astype(BF16), blocks, L1_FL // MXU_COLS


def _tiles(seq):
    return dict(
        tm_in=min(256, seq), tm_out=min(512, seq),
        dsa=dict(tq=min(256, seq), kc=min(512, seq)),
        stick=dict(tq=min(256, seq), ck=128, grp=2, hg=N_HEADS_C),
        fox=dict(tq=min(512, seq), kc=min(512, seq), hg=N_HEADS_D // 2),
    )


def kernel(x, norm0, w_in0, w_out0, norm1, w_in1, b_f1, w_out1, norm_f):
    b, s, d = x.shape
    m = b * s
    x2 = x.reshape(m, d)
    tiles = _tiles(s)
    tm_in, tm_out = tiles["tm_in"], tiles["tm_out"]

    w0, blocks0, small0 = _layer0_weights(w_in0)
    main0, small0_out = _inproj(x2, norm0, w0, _rope_tables(s), blocks0, small0, s, tm_in)
    main0_3 = main0.reshape(b, s, L0_WIDTH)
    iwt = small0_out[:, IDX_DIM:IDX_DIM + IDX_HEADS].reshape(b, s, IDX_HEADS).transpose(0, 2, 1)
    y_a = _dsa(main0_3, iwt, **tiles["dsa"])
    y_b = _dilated(main0_3)
    x2 = _outproj(x2, y_a.reshape(m, -1), y_b.reshape(m, -1), main0, w_out0.astype(BF16), norm_f,
                  final_norm=False, tm=tm_out)

    w1, blocks1, small1 = _layer1_weights(w_in1)
    main1, small1_out = _inproj(x2, norm1, w1, None, blocks1, small1, s, tm_in)
    main1_3 = main1.reshape(b, s, L1_WIDTH)
    y_c = _stick(main1_3, **tiles["stick"])
    fl = small1_out[:, :N_HEADS_D].reshape(b, s, N_HEADS_D).transpose(0, 2, 1)
    c = _forget_cumsum(fl.reshape(b, N_HEADS_D, s // LANES, LANES),
                       jnp.broadcast_to(b_f1.astype(F32)[:, None, None], (N_HEADS_D, 1, LANES)))
    crow = c.reshape(b, N_HEADS_D, 1, s)
    y_d = _fox(main1_3, crow, **tiles["fox"])
    out = _outproj(x2, y_c.reshape(m, -1), y_d.reshape(m, -1), main1, w_out1.astype(BF16), norm_f,
                   final_norm=True, tm=tm_out)
    return out.reshape(b, s, d)
```
